```python
import math
import jax, jax.numpy as jnp
from jax import lax
import numpy as np

D_MODEL = 1024
BATCH = 8
SEQ = 4096
DEPTH = 4

GRID_W = 64
D_MIX = D_MODEL
N_ATT_HEADS = 8
HEAD_DIM = 64
D_ATT = N_ATT_HEADS * HEAD_DIM
D_LRU = D_MIX - D_ATT
N_LRU_BLOCKS = 8
LRU_BLOCK = D_LRU // N_LRU_BLOCKS
WIN_H_MAX = 8
WIN_W = 16
CONV_LRU = 4
LRU_PAD = (2, 1)
LRU_C = 8.0
D_FF = 3 * D_MODEL
CONV_FFN = 3
D_IN = 3 * D_ATT + 2 * D_LRU
EPS = 1e-6

kernel_name = "hymba_natten_rglru_convffn_encoder"


def rmsnorm(x, g):
    xf = x.astype(jnp.float32)
    y = xf * lax.rsqrt(jnp.mean(xf * xf, axis=-1, keepdims=True) + EPS)
    return (y * g.astype(jnp.float32)).astype(x.dtype)


def depthwise_conv(x, w, b, pad):
    c = x.shape[-1]
    y = lax.conv_general_dilated(
        x, w[:, None, :].astype(x.dtype), window_strides=(1,), padding=[pad],
        dimension_numbers=("NWC", "WIO", "NWC"), feature_group_count=c)
    return y + b.astype(x.dtype)


def neighbourhood_attention(q, k, v, rel_bias):
    bsz, s, h, dh = q.shape
    rows = s // GRID_W
    kh = min(WIN_H_MAX, rows)
    qg = q.reshape(bsz, rows, GRID_W, h, dh)
    kg = k.reshape(bsz, rows, GRID_W, h, dh)
    vg = v.reshape(bsz, rows, GRID_W, h, dh)
    col = jnp.arange(GRID_W)
    col_start = jnp.clip(col - WIN_W // 2, 0, GRID_W - WIN_W)
    key_cols = col_start[:, None] + jnp.arange(WIN_W)[None, :]
    dc = key_cols - col[:, None] + (WIN_W - 1)
    scale = dh ** -0.5

    def row_block(args):
        r, q_row = args
        row_start = jnp.clip(r - kh // 2, 0, rows - kh)
        k_band = lax.dynamic_slice_in_dim(kg, row_start, kh, axis=1)
        v_band = lax.dynamic_slice_in_dim(vg, row_start, kh, axis=1)
        k_win = k_band[:, :, key_cols]
        v_win = v_band[:, :, key_cols]
        sc = jnp.einsum("bqhd,biqjhd->bhqij", q_row, k_win,
                        preferred_element_type=jnp.float32) * scale
        dr = row_start + jnp.arange(kh) - r + (WIN_H_MAX - 1)
        bias = rel_bias[:, dr[None, :, None], dc[:, None, :]]
        sc = sc + bias.astype(jnp.float32)[None]
        p = jax.nn.softmax(sc.reshape(bsz, h, GRID_W, kh * WIN_W), axis=-1)
        p = p.reshape(bsz, h, GRID_W, kh, WIN_W).astype(v.dtype)
        return jnp.einsum("bhqij,biqjhd->bqhd", p, v_win)

    out = lax.map(row_block, (jnp.arange(rows), jnp.moveaxis(qg, 1, 0)))
    return jnp.moveaxis(out, 0, 1).reshape(bsz, s, h * dh)


def linear_scan(a, u, reverse):
    def combine(left, right):
        a_l, u_l = left
        a_r, u_r = right
        return a_l * a_r, a_r * u_l + u_r
    _, hs = lax.associative_scan(combine, (a, u), reverse=reverse, axis=1)
    return hs


def rg_lru(x, w_gate, b_gate, lam, reverse):
    bsz, s, _ = x.shape
    xb = x.reshape(bsz, s, N_LRU_BLOCKS, LRU_BLOCK)
    g = jnp.einsum("bsnc,gncd->gbsnd", xb, w_gate).reshape(2, bsz, s, D_LRU)
    g = g.astype(jnp.float32) + b_gate.astype(jnp.float32)[:, None, None, :]
    r_gate = jax.nn.sigmoid(g[0])
    i_gate = jax.nn.sigmoid(g[1])
    log_a = -LRU_C * r_gate * jax.nn.softplus(-lam.astype(jnp.float32))
    a = jnp.exp(log_a)
    u = jnp.sqrt(-jnp.expm1(2.0 * log_a)) * (i_gate * x.astype(jnp.float32))
    return linear_scan(a, u, reverse).astype(x.dtype)


def setup_inputs(seed: int = 0) -> dict:
    key = jax.random.key(seed)
    ks = jax.random.split(key, 20)
    f32 = jnp.float32
    nrm = lambda k, shape, s: jax.random.normal(k, shape, f32) * s
    a_c = jax.random.uniform(ks[8], (DEPTH, 2, D_LRU), f32, 0.9, 0.999)
    a0 = a_c ** (1.0 / LRU_C)
    lru_lam = jnp.log(a0) - jnp.log1p(-a0)
    return {
        "x": nrm(ks[0], (BATCH, SEQ, D_MODEL), 1.0),
        "g_mix": 1.0 + nrm(ks[1], (DEPTH, D_MODEL), 0.02),
        "w_in": nrm(ks[2], (DEPTH, D_MODEL, D_IN), D_MODEL ** -0.5),
        "rel_bias": nrm(ks[3], (DEPTH, N_ATT_HEADS, 2 * WIN_H_MAX - 1, 2 * WIN_W - 1), 0.5),
        "conv_lru_w": nrm(ks[4], (DEPTH, CONV_LRU, D_LRU), CONV_LRU ** -0.5),
        "conv_lru_b": nrm(ks[5], (DEPTH, D_LRU), 0.01),
        "lru_w": nrm(ks[6], (DEPTH, 2, 2, N_LRU_BLOCKS, LRU_BLOCK, LRU_BLOCK), LRU_BLOCK ** -0.5),
        "lru_b": nrm(ks[7], (DEPTH, 2, 2, D_LRU), 0.1),
        "lru_lam": lru_lam,
        "g_att": 1.0 + nrm(ks[9], (DEPTH, D_ATT), 0.02),
        "g_rec": 1.0 + nrm(ks[10], (DEPTH, D_LRU), 0.02),
        "w_out": nrm(ks[11], (DEPTH, D_MIX, D_MODEL), (2.0 * D_MIX) ** -0.5),
        "g_ffn": 1.0 + nrm(ks[12], (DEPTH, D_MODEL), 0.02),
        "w_up": nrm(ks[13], (DEPTH, D_MODEL, 2 * D_FF), D_MODEL ** -0.5),
        "conv_ffn_w": nrm(ks[14], (DEPTH, CONV_FFN, D_FF), CONV_FFN ** -0.5),
        "conv_ffn_b": nrm(ks[15], (DEPTH, D_FF), 0.01),
        "w_down": nrm(ks[16], (DEPTH, D_FF, D_MODEL), (2.0 * D_FF) ** -0.5),
        "g_final": 1.0 + nrm(ks[17], (D_MODEL,), 0.02),
    }


def reference(x, g_mix, w_in, rel_bias, conv_lru_w, conv_lru_b, lru_w, lru_b, lru_lam,
              g_att, g_rec, w_out, g_ffn, w_up, conv_ffn_w, conv_ffn_b, w_down, g_final):
    bsz, s, _ = x.shape
    for l in range(DEPTH):
        h = rmsnorm(x, g_mix[l])
        z = h @ w_in[l]
        q, k, v, x_rec, gate = jnp.split(
            z, [D_ATT, 2 * D_ATT, 3 * D_ATT, 3 * D_ATT + D_LRU], axis=-1)
        att = neighbourhood_attention(
            q.reshape(bsz, s, N_ATT_HEADS, HEAD_DIM),
            k.reshape(bsz, s, N_ATT_HEADS, HEAD_DIM),
            v.reshape(bsz, s, N_ATT_HEADS, HEAD_DIM), rel_bias[l])
        xc = depthwise_conv(x_rec, conv_lru_w[l], conv_lru_b[l], LRU_PAD)
        rec_f = rg_lru(xc, lru_w[l, 0], lru_b[l, 0], lru_lam[l, 0], reverse=False)
        rec_b = rg_lru(xc, lru_w[l, 1], lru_b[l, 1], lru_lam[l, 1], reverse=True)
        rec = (rec_f + rec_b) * jax.nn.gelu(gate, approximate=True)
        mixed = jnp.concatenate([rmsnorm(att, g_att[l]), rmsnorm(rec, g_rec[l])], axis=-1)
        x = x + mixed @ w_out[l]
        h = rmsnorm(x, g_ffn[l])
        u = h @ w_up[l]
        u_act, u_lin = jnp.split(u, 2, axis=-1)
        u_act = depthwise_conv(u_act, conv_ffn_w[l], conv_ffn_b[l], (1, 1))
        x = x + (jax.nn.gelu(u_act, approximate=True) * u_lin) @ w_down[l]
    return rmsnorm(x, g_final)
```

```python
import functools
import math

import jax
import jax.numpy as jnp
from jax import lax
from jax.experimental import pallas as pl
from jax.experimental.pallas import tpu as pltpu

F32 = jnp.float32
BF16 = jnp.bfloat16

GRID_W = 64
N_HEADS = 8
HEAD_DIM = 64
D_ATT = N_HEADS * HEAD_DIM
WIN_H = 8
WIN_W = 16
LRU_C = 8.0
EPS = 1e-6
MASKED = -1e30
VMEM_LIMIT = 56 * 1024 * 1024


def _rms(x, g):
    ms = jnp.mean(x * x, axis=-1, keepdims=True)
    return x * lax.rsqrt(ms + EPS) * g


def _gelu(x):
    c = math.sqrt(2.0 / math.pi)
    return 0.5 * x * (1.0 + jnp.tanh(c * (x + 0.044715 * (x * x * x))))


def _params(sem):
    return pltpu.CompilerParams(dimension_semantics=sem, vmem_limit_bytes=VMEM_LIMIT)


def _in_proj_kernel(x_ref, g_ref, w_ref, qkv_ref, rg_ref):
    hn = _rms(x_ref[...], g_ref[...]).astype(BF16)
    n_qkv = qkv_ref.shape[1]
    q = jnp.dot(hn, w_ref[:, :D_ATT], preferred_element_type=F32)
    qkv_ref[:, :D_ATT] = (q * (HEAD_DIM ** -0.5)).astype(BF16)
    kv = jnp.dot(hn, w_ref[:, D_ATT:n_qkv], preferred_element_type=F32)
    qkv_ref[:, D_ATT:] = kv.astype(BF16)
    rg_ref[...] = jnp.dot(hn, w_ref[:, n_qkv:], preferred_element_type=F32)


def _in_proj(x, g, w, tm=512):
    n, d = x.shape
    d_in = w.shape[1]
    n_qkv = 3 * D_ATT
    return pl.pallas_call(
        _in_proj_kernel,
        grid=(n // tm,),
        in_specs=[
            pl.BlockSpec((tm, d), lambda i: (i, 0)),
            pl.BlockSpec((1, d), lambda i: (0, 0)),
            pl.BlockSpec((d, d_in), lambda i: (0, 0)),
        ],
        out_specs=[
            pl.BlockSpec((tm, n_qkv), lambda i: (i, 0)),
            pl.BlockSpec((tm, d_in - n_qkv), lambda i: (i, 0)),
        ],
        out_shape=[
            jax.ShapeDtypeStruct((n, n_qkv), BF16),
            jax.ShapeDtypeStruct((n, d_in - n_qkv), F32),
        ],
        compiler_params=_params(("parallel",)),
        name="in_proj",
    )(x, g.reshape(1, d), w)


def _bias_table(rel_bias):
    col = jnp.arange(GRID_W)
    col_start = jnp.clip(col - WIN_W // 2, 0, GRID_W - WIN_W)
    dr = jnp.arange(WIN_H)[None, :] - jnp.arange(WIN_H)[:, None] + (WIN_H - 1)
    dc = col[None, :] - col[:, None] + (WIN_W - 1)
    valid = (col[None, :] >= col_start[:, None]) & (col[None, :] < col_start[:, None] + WIN_W)
    tab = rel_bias[:, dr[:, None, :, None], jnp.clip(dc, 0, 2 * WIN_W - 2)[None, :, None, :]]
    tab = jnp.where(valid[None, None, :, None, :], tab.astype(F32), MASKED)
    return jnp.transpose(tab, (1, 0, 2, 3, 4)).reshape(WIN_H, N_HEADS, GRID_W, WIN_H * GRID_W)


def _attn_kernel(q_ref, k_ref, v_ref, bias_ref, g_ref, o_ref, *, rows, rows_per_step):
    i = pl.program_id(1)
    band = WIN_H * GRID_W
    low_half = lax.broadcasted_iota(jnp.int32, (GRID_W, 2 * HEAD_DIM), 1) < HEAD_DIM
    g = g_ref[...]

    def row_body(j, carry):
        r = i * rows_per_step + j
        row_start = jnp.clip(r - WIN_H // 2, 0, rows - WIN_H)
        off = r - row_start
        k0 = pl.multiple_of(row_start * GRID_W, GRID_W)
        q0 = pl.multiple_of(j * GRID_W, GRID_W)
        outs = []
        for p in range(N_HEADS // 2):
            lanes = pl.ds(2 * HEAD_DIM * p, 2 * HEAD_DIM)
            qp = q_ref[0, pl.ds(q0, GRID_W), lanes]
            kp = k_ref[0, pl.ds(k0, band), lanes]
            vp = v_ref[0, pl.ds(k0, band), lanes]
            halves = []
            for s in range(2):
                keep = low_half if s == 0 else jnp.logical_not(low_half)
                qm = jnp.where(keep, qp, jnp.zeros_like(qp))
                sc = lax.dot_general(qm, kp, (((1,), (1,)), ((), ())),
                                     preferred_element_type=F32)
                sc = sc + bias_ref[off, 2 * p + s]
                m = jnp.max(sc, axis=-1, keepdims=True)
                e = jnp.exp(sc - m)
                denom = jnp.sum(e, axis=-1, keepdims=True)
                pv = jnp.dot(e.astype(BF16), vp, preferred_element_type=F32)
                halves.append(pv / denom)
            outs.append(jnp.where(low_half, halves[0], halves[1]))
        att = jnp.concatenate(outs, axis=-1)
        o_ref[0, pl.ds(q0, GRID_W), :] = _rms(att, g).astype(o_ref.dtype)
        return carry

    lax.fori_loop(0, rows_per_step, row_body, 0)


def _attention(qkv, bias_tab, g_att, rows_per_step=8):
    bsz, s, _ = qkv.shape
    rows = s // GRID_W
    tq = rows_per_step * GRID_W
    kern = functools.partial(_attn_kernel, rows=rows, rows_per_step=rows_per_step)
    return pl.pallas_call(
        kern,
        grid=(bsz, rows // rows_per_step),
        in_specs=[
            pl.BlockSpec((1, tq, D_ATT), lambda b, i: (b, i, 0)),
            pl.BlockSpec((1, s, D_ATT), lambda b, i: (b, 0, 1)),
            pl.BlockSpec((1, s, D_ATT), lambda b, i: (b, 0, 2)),
            pl.BlockSpec(bias_tab.shape, lambda b, i: (0, 0, 0, 0)),
            pl.BlockSpec((1, D_ATT), lambda b, i: (0, 0)),
        ],
        out_specs=pl.BlockSpec((1, tq, D_ATT), lambda b, i: (b, i, 0)),
        out_shape=jax.ShapeDtypeStruct((bsz, s, D_ATT), BF16),
        compiler_params=_params(("parallel", "arbitrary")),
        name="nbr_attention",
    )(qkv, qkv, qkv, bias_tab, g_att.reshape(1, D_ATT))


def _lru_au(xe_ref, a_ref, u_ref, cw_ref, cb_ref, wg_ref, bg_ref, lam_ref, *, bsz, tc, sub_t):
    c = a_ref.shape[1]
    z = -lam_ref[...]
    neg_c_softplus = -LRU_C * (jnp.maximum(z, 0.0) + jnp.log1p(jnp.exp(-jnp.abs(z))))
    sub = sub_t * bsz

    def body(sb, carry):
        r0 = pl.multiple_of(sb * sub, sub)
        xc = cb_ref[...] + cw_ref[0:1, :] * xe_ref[pl.ds(r0, sub), :]
        for j in range(1, cw_ref.shape[0]):
            xc = xc + cw_ref[j:j + 1, :] * xe_ref[pl.ds(r0 + j * bsz, sub), :]
        gates = jnp.dot(xc.astype(BF16), wg_ref[...], preferred_element_type=F32) + bg_ref[...]
        r_gate = jax.nn.sigmoid(gates[:, :c])
        i_gate = jax.nn.sigmoid(gates[:, c:])
        log_a = r_gate * neg_c_softplus
        a = jnp.exp(log_a)
        a_ref[pl.ds(r0, sub), :] = a
        one_minus_a2 = -jnp.tanh(log_a) * (a * a + 1.0)
        u_ref[pl.ds(r0, sub), :] = jnp.sqrt(one_minus_a2) * (i_gate * xc)
        return carry

    lax.fori_loop(0, tc // sub_t, body, 0)


def _fill_halo(xe_ref, xp_ref, xc_ref, xn_ref, has_prev, has_next, *, bsz, tc, pad):
    left, right = pad
    hb = xp_ref.shape[0]
    prev = xp_ref[pl.ds(hb - left * bsz, left * bsz), :]
    xe_ref[pl.ds(0, left * bsz), :] = jnp.where(has_prev, prev, 0.0)
    xe_ref[pl.ds(left * bsz, tc * bsz), :] = xc_ref[...]
    nxt = xn_ref[pl.ds(0, right * bsz), :]
    xe_ref[pl.ds((left + tc) * bsz, right * bsz), :] = jnp.where(has_next, nxt, 0.0)


def _lru_fwd_kernel(xp_ref, xc_ref, xn_ref, cw_ref, cb_ref, wg_ref, bg_ref, lam_ref,
                    hf_ref, xe_ref, a_ref, u_ref, h_ref, *, bsz, tc, sub_t, pad, unroll):
    i = pl.program_id(0)
    nc = pl.num_programs(0)

    @pl.when(i == 0)
    def _():
        h_ref[...] = jnp.zeros_like(h_ref)

    _fill_halo(xe_ref, xp_ref, xc_ref, xn_ref, i > 0, i < nc - 1, bsz=bsz, tc=tc, pad=pad)
    _lru_au(xe_ref, a_ref, u_ref, cw_ref, cb_ref, wg_ref, bg_ref, lam_ref,
            bsz=bsz, tc=tc, sub_t=sub_t)

    def step(t, h):
        rows = pl.ds(pl.multiple_of(t * bsz, bsz), bsz)
        h = a_ref[rows, :] * h + u_ref[rows, :]
        hf_ref[rows, :] = h
        return h

    h_ref[...] = lax.fori_loop(0, tc, step, h_ref[...], unroll=unroll)


def _lru_bwd_kernel(xp_ref, xc_ref, xn_ref, gate_ref, hf_ref, cw_ref, cb_ref, wg_ref, bg_ref,
                    lam_ref, grec_ref, o_ref, xe_ref, a_ref, u_ref, h_ref,
                    *, bsz, tc, sub_t, pad, unroll):
    i = pl.program_id(0)
    nc = pl.num_programs(0)

    @pl.when(i == 0)
    def _():
        h_ref[...] = jnp.zeros_like(h_ref)

    _fill_halo(xe_ref, xp_ref, xc_ref, xn_ref, i < nc - 1, i > 0, bsz=bsz, tc=tc, pad=pad)
    _lru_au(xe_ref, a_ref, u_ref, cw_ref, cb_ref, wg_ref, bg_ref, lam_ref,
            bsz=bsz, tc=tc, sub_t=sub_t)

    def step(k, h):
        t = tc - 1 - k
        rows = pl.ds(pl.multiple_of(t * bsz, bsz), bsz)
        h = a_ref[rows, :] * h + u_ref[rows, :]
        u_ref[rows, :] = h
        return h

    h_ref[...] = lax.fori_loop(0, tc, step, h_ref[...], unroll=unroll)

    sub = sub_t * bsz

    def finish(sb, carry):
        rows = pl.ds(pl.multiple_of(sb * sub, sub), sub)
        rec = (hf_ref[rows, :] + u_ref[rows, :]) * _gelu(gate_ref[rows, :])
        o_ref[rows, :] = _rms(rec, grec_ref[...]).astype(o_ref.dtype)
        return carry

    lax.fori_loop(0, tc // sub_t, finish, 0)


def _lru(rg_t, cw, cb, wg, bg, lam, g_rec, *, bsz, tc=256, sub_t=32, unroll=8):
    rows, c2 = rg_t.shape
    c = c2 // 2
    s = rows // bsz
    nc = s // tc
    k = cw.shape[0]
    pad = (k // 2, k - 1 - k // 2)
    halo_t = 8
    hb = halo_t * bsz
    n_halo = s // halo_t
    cpb = tc // halo_t
    common = dict(bsz=bsz, tc=tc, sub_t=sub_t, pad=pad, unroll=unroll)
    scratch = [
        pltpu.VMEM(((tc + k - 1) * bsz, c), F32),
        pltpu.VMEM((tc * bsz, c), F32),
        pltpu.VMEM((tc * bsz, c), F32),
        pltpu.VMEM((bsz, c), F32),
    ]

    def x_specs(chunk):
        return [
            pl.BlockSpec((hb, c), lambda i: (jnp.maximum(chunk(i) * cpb - 1, 0), 0)),
            pl.BlockSpec((tc * bsz, c), lambda i: (chunk(i), 0)),
            pl.BlockSpec((hb, c), lambda i: (jnp.minimum((chunk(i) + 1) * cpb, n_halo - 1), 0)),
        ]

    def p_specs():
        return [
            pl.BlockSpec((k, c), lambda i: (0, 0)),
            pl.BlockSpec((1, c), lambda i: (0, 0)),
            pl.BlockSpec((c, 2 * c), lambda i: (0, 0)),
            pl.BlockSpec((1, 2 * c), lambda i: (0, 0)),
            pl.BlockSpec((1, c), lambda i: (0, 0)),
        ]

    fwd = lambda i: i
    hf = pl.pallas_call(
        functools.partial(_lru_fwd_kernel, **common),
        grid=(nc,),
        in_specs=x_specs(fwd) + p_specs(),
        out_specs=pl.BlockSpec((tc * bsz, c), lambda i: (i, 0)),
        out_shape=jax.ShapeDtypeStruct((rows, c), F32),
        scratch_shapes=scratch,
        compiler_params=_params(("arbitrary",)),
        name="lru_fwd",
    )(rg_t, rg_t, rg_t, cw, cb.reshape(1, c), wg[0], bg[0].reshape(1, 2 * c), lam[0].reshape(1, c))

    bwd = lambda i: nc - 1 - i
    return pl.pallas_call(
        functools.partial(_lru_bwd_kernel, **common),
        grid=(nc,),
        in_specs=x_specs(bwd) + [
            pl.BlockSpec((tc * bsz, c), lambda i: (bwd(i), 1)),
            pl.BlockSpec((tc * bsz, c), lambda i: (bwd(i), 0)),
        ] + p_specs() + [pl.BlockSpec((1, c), lambda i: (0, 0))],
        out_specs=pl.BlockSpec((tc * bsz, c), lambda i: (bwd(i), 0)),
        out_shape=jax.ShapeDtypeStruct((rows, c), BF16),
        scratch_shapes=scratch,
        compiler_params=_params(("arbitrary",)),
        name="lru_bwd",
    )(rg_t, rg_t, rg_t, rg_t, hf, cw, cb.reshape(1, c), wg[1], bg[1].reshape(1, 2 * c),
      lam[1].reshape(1, c), g_rec.reshape(1, c))


def _gate_weights(lru_w):
    nd, ng, nb, bs, _ = lru_w.shape
    eye = jnp.eye(nb, dtype=lru_w.dtype)
    dense = jnp.einsum("xgncd,nm->xgncmd", lru_w, eye).reshape(nd, ng, nb * bs, nb * bs)
    return jnp.concatenate([dense[:, 0], dense[:, 1]], axis=-1)


def _out_up_kernel(x_ref, att_ref, rec_ref, wo_ref, g_ref, wu_ref, x1_ref, u_ref, *, n_chunks):
    mixed = jnp.concatenate([att_ref[...], rec_ref[...]], axis=-1)
    x1 = x_ref[...] + jnp.dot(mixed, wo_ref[...], preferred_element_type=F32)
    x1_ref[...] = x1
    hn = _rms(x1, g_ref[...]).astype(BF16)
    wc = u_ref.shape[1] // n_chunks
    for j in range(n_chunks):
        cols = slice(j * wc, (j + 1) * wc)
        u_ref[:, cols] = jnp.dot(hn, wu_ref[:, cols], preferred_element_type=F32).astype(u_ref.dtype)


def _out_up(x, att, rec, wo, g, wu, tm=256):
    n, d = x.shape
    c = att.shape[1]
    f2 = wu.shape[1]
    return pl.pallas_call(
        functools.partial(_out_up_kernel, n_chunks=4),
        grid=(n // tm,),
        in_specs=[
            pl.BlockSpec((tm, d), lambda i: (i, 0)),
            pl.BlockSpec((tm, c), lambda i: (i, 0)),
            pl.BlockSpec((tm, c), lambda i: (i, 0)),
            pl.BlockSpec(wo.shape, lambda i: (0, 0)),
            pl.BlockSpec((1, d), lambda i: (0, 0)),
            pl.BlockSpec(wu.shape, lambda i: (0, 0)),
        ],
        out_specs=[
            pl.BlockSpec((tm, d), lambda i: (i, 0)),
            pl.BlockSpec((tm, f2), lambda i: (i, 0)),
        ],
        out_shape=[
            jax.ShapeDtypeStruct((n, d), F32),
            jax.ShapeDtypeStruct((n, f2), BF16),
        ],
        compiler_params=_params(("parallel",)),
        name="out_up_proj",
    )(x, att, rec, wo, g.reshape(1, d), wu)


def _ffn_down_kernel(x1_ref, ua_ref, up_ref, un_ref, ul_ref, cw_ref, cb_ref, wd_ref, gf_ref, o_ref,
                     *, n_chunks, final_norm):
    j = pl.program_id(1)
    nj = pl.num_programs(1)
    tt = ua_ref.shape[1]
    hb = up_ref.shape[1]
    wc = ua_ref.shape[2] // n_chunks
    row = lax.broadcasted_iota(jnp.int32, (tt, wc), 0)
    acc = x1_ref[0]
    for ch in range(n_chunks):
        cols = pl.ds(ch * wc, wc)
        ua = ua_ref[0, :, cols].astype(F32)
        prev_row = jnp.where(j > 0, up_ref[0, hb - 1:hb, cols].astype(F32), 0.0)
        next_row = jnp.where(j < nj - 1, un_ref[0, 0:1, cols].astype(F32), 0.0)
        before = jnp.where(row == 0, prev_row, pltpu.roll(ua, 1, axis=0))
        after = jnp.where(row == tt - 1, next_row, pltpu.roll(ua, tt - 1, axis=0))
        conv = (cb_ref[:, cols] + cw_ref[0:1, cols] * before + cw_ref[1:2, cols] * ua
                + cw_ref[2:3, cols] * after)
        act = (_gelu(conv) * ul_ref[0, :, cols].astype(F32)).astype(BF16)
        acc = acc + jnp.dot(act, wd_ref[cols, :], preferred_element_type=F32)
    if final_norm:
        acc = _rms(acc, gf_ref[...])
    o_ref[0] = acc


def _ffn_down(x1, u, cw, cb, wd, g_final, *, final_norm, tt=256):
    bsz, s, d = x1.shape
    f = wd.shape[0]
    hb = 16
    tpb = tt // hb
    n_hb = s // hb
    return pl.pallas_call(
        functools.partial(_ffn_down_kernel, n_chunks=6, final_norm=final_norm),
        grid=(bsz, s // tt),
        in_specs=[
            pl.BlockSpec((1, tt, d), lambda b, j: (b, j, 0)),
            pl.BlockSpec((1, tt, f), lambda b, j: (b, j, 0)),
            pl.BlockSpec((1, hb, f), lambda b, j: (b, jnp.maximum(j * tpb - 1, 0), 0)),
            pl.BlockSpec((1, hb, f), lambda b, j: (b, jnp.minimum((j + 1) * tpb, n_hb - 1), 0)),
            pl.BlockSpec((1, tt, f), lambda b, j: (b, j, 1)),
            pl.BlockSpec(cw.shape, lambda b, j: (0, 0)),
            pl.BlockSpec((1, f), lambda b, j: (0, 0)),
            pl.BlockSpec(wd.shape, lambda b, j: (0, 0)),
            pl.BlockSpec((1, d), lambda b, j: (0, 0)),
        ],
        out_specs=pl.BlockSpec((1, tt, d), lambda b, j: (b, j, 0)),
        out_shape=jax.ShapeDtypeStruct((bsz, s, d), F32),
        compiler_params=_params(("parallel", "arbitrary")),
        name="ffn_down",
    )(x1, u, u, u, u, cw, cb.reshape(1, f), wd, g_final.reshape(1, d))


def kernel(x, g_mix, w_in, rel_bias, conv_lru_w, conv_lru_b, lru_w, lru_b, lru_lam, g_att, g_rec,
           w_out, g_ffn, w_up, conv_ffn_w, conv_ffn_b, w_down, g_final):
    bsz, s, d = x.shape
    depth = w_in.shape[0]
    n = bsz * s
    c_lru = conv_lru_w.shape[-1]
    w_in_b = w_in.astype(BF16)
    w_out_b = w_out.astype(BF16)
    w_up_b = w_up.astype(BF16)
    w_down_b = w_down.astype(BF16)
    for l in range(depth):
        qkv, rg = _in_proj(x.reshape(n, d), g_mix[l], w_in_b[l])
        att = _attention(qkv.reshape(bsz, s, 3 * D_ATT), _bias_table(rel_bias[l]), g_att[l])
        rg_t = jnp.transpose(rg.reshape(bsz, s, 2 * c_lru), (1, 0, 2)).reshape(s * bsz, 2 * c_lru)
        wg = _gate_weights(lru_w[l]).astype(BF16)
        bg = jnp.concatenate([lru_b[l, :, 0], lru_b[l, :, 1]], axis=-1)
        rec_t = _lru(rg_t, conv_lru_w[l], conv_lru_b[l], wg, bg, lru_lam[l], g_rec[l], bsz=bsz)
        rec = jnp.transpose(rec_t.reshape(s, bsz, c_lru), (1, 0, 2)).reshape(n, c_lru)
        x1, u = _out_up(x.reshape(n, d), att.reshape(n, D_ATT), rec, w_out_b[l], g_ffn[l], w_up_b[l])
        x = _ffn_down(x1.reshape(bsz, s, d), u.reshape(bsz, s, -1), conv_ffn_w[l], conv_ffn_b[l],
                      w_down_b[l], g_final, final_norm=(l == depth - 1))
    return x
```

```python
import functools
import math

import jax
import jax.numpy as jnp
from jax import lax
from jax.experimental import pallas as pl
from jax.experimental.pallas import tpu as pltpu

F32 = jnp.float32
BF16 = jnp.bfloat16

GRID_W = 64
N_HEADS = 8
HEAD_DIM = 64
D_ATT = N_HEADS * HEAD_DIM
WIN_H = 8
WIN_W = 16
LRU_C = 8.0
EPS = 1e-6
MASKED = -1e30
VMEM_LIMIT = 56 * 1024 * 1024


def _rms(x, g):
    ms = jnp.mean(x * x, axis=-1, keepdims=True)
    return x * lax.rsqrt(ms + EPS) * g


def _gelu(x):
    c = math.sqrt(2.0 / math.pi)
    return 0.5 * x * (1.0 + jnp.tanh(c * (x + 0.044715 * (x * x * x))))


def _params(sem):
    return pltpu.CompilerParams(dimension_semantics=sem, vmem_limit_bytes=VMEM_LIMIT)


def _in_proj_kernel(x_ref, g_ref, w_ref, qk_ref, vx_ref, rg_ref):
    hn = _rms(x_ref[...], g_ref[...]).astype(BF16)
    pair = 2 * HEAD_DIM
    q = jnp.dot(hn, w_ref[:, :D_ATT], preferred_element_type=F32)
    qk_ref[:, :D_ATT] = (q * (HEAD_DIM ** -0.5)).astype(BF16)
    k = jnp.dot(hn, w_ref[:, D_ATT:2 * D_ATT], preferred_element_type=F32)
    qk_ref[:, D_ATT:] = k.astype(BF16)
    v = jnp.dot(hn, w_ref[:, 2 * D_ATT:3 * D_ATT], preferred_element_type=F32).astype(BF16)
    ones = jnp.ones((v.shape[0], pair), BF16)
    for p in range(D_ATT // pair):
        vx_ref[:, 2 * p * pair:(2 * p + 1) * pair] = v[:, p * pair:(p + 1) * pair]
        vx_ref[:, (2 * p + 1) * pair:(2 * p + 2) * pair] = ones
    rg_ref[...] = jnp.dot(hn, w_ref[:, 3 * D_ATT:], preferred_element_type=F32)


def _in_proj(x, g, w, tm=512):
    n, d = x.shape
    d_in = w.shape[1]
    d_rg = d_in - 3 * D_ATT
    return pl.pallas_call(
        _in_proj_kernel,
        grid=(n // tm,),
        in_specs=[
            pl.BlockSpec((tm, d), lambda i: (i, 0)),
            pl.BlockSpec((1, d), lambda i: (0, 0)),
            pl.BlockSpec((d, d_in), lambda i: (0, 0)),
        ],
        out_specs=[
            pl.BlockSpec((tm, 2 * D_ATT), lambda i: (i, 0)),
            pl.BlockSpec((tm, 2 * D_ATT), lambda i: (i, 0)),
            pl.BlockSpec((tm, d_rg), lambda i: (i, 0)),
        ],
        out_shape=[
            jax.ShapeDtypeStruct((n, 2 * D_ATT), BF16),
            jax.ShapeDtypeStruct((n, 2 * D_ATT), BF16),
            jax.ShapeDtypeStruct((n, d_rg), F32),
        ],
        compiler_params=_params(("parallel",)),
        name="in_proj",
    )(x, g.reshape(1, d), w)


def _bias_rows(rel_bias):
    h = rel_bias.shape[0]
    z = jnp.zeros((h, 2 * WIN_H - 2, 33), rel_bias.dtype)
    return jnp.concatenate(
        [rel_bias[:, :-1, WIN_W - 1:], z, rel_bias[:, 1:, :], z, rel_bias[:, :-1, :WIN_W - 1]], axis=-1)


def _bias_kernel(rows_ref, o_ref):
    n_dr = o_ref.shape[0]
    shape = (GRID_W, 2 * GRID_W)
    c = lax.broadcasted_iota(jnp.int32, shape, 0)
    kc = lax.broadcasted_iota(jnp.int32, shape, 1) & (GRID_W - 1)
    col_start = jnp.clip(c - WIN_W // 2, 0, GRID_W - WIN_W)
    valid = (kc >= col_start) & (kc < col_start + WIN_W)
    for h in range(N_HEADS):
        for dr in range(n_dr):
            row = jnp.broadcast_to(rows_ref[h, dr:dr + 1, :], shape)
            toeplitz = pltpu.roll(row, 0, 1, stride=1, stride_axis=0)
            o_ref[dr, h * GRID_W:(h + 1) * GRID_W, :] = jnp.where(valid, toeplitz, MASKED)


def _bias_pieces(rel_bias):
    rows = _bias_rows(rel_bias.astype(F32))
    n_dr = rows.shape[1]
    return pl.pallas_call(
        _bias_kernel,
        out_shape=jax.ShapeDtypeStruct((n_dr, N_HEADS * GRID_W, 2 * GRID_W), F32),
        name="bias_pieces",
    )(rows)


def _attn_kernel(q_ref, k_ref, v_ref, bias_ref, g_ref, o_ref, *, rows, rows_per_step, rows_per_iter):
    i = pl.program_id(1)
    band = WIN_H * GRID_W
    pair = 2 * HEAD_DIM
    group = 4 * HEAD_DIM
    n_groups = D_ATT // group
    lane = lax.broadcasted_iota(jnp.int32, (GRID_W, group), 1)
    head_lanes = [(lane >= h * HEAD_DIM) & (lane < (h + 1) * HEAD_DIM) for h in range(4)]
    low_half = lax.broadcasted_iota(jnp.int32, (GRID_W, pair), 1) < HEAD_DIM
    g = g_ref[...]

    def qk_stage(r, q0):
        row_start = jnp.clip(r - WIN_H // 2, 0, rows - WIN_H)
        off = r - row_start
        k0 = pl.multiple_of(row_start * GRID_W, GRID_W)
        scores = []
        for gi in range(n_groups):
            lanes = pl.ds(gi * group, group)
            q4 = q_ref[0, pl.ds(q0, GRID_W), lanes]
            qm = jnp.concatenate([jnp.where(m, q4, jnp.zeros_like(q4)) for m in head_lanes], axis=0)
            k4 = k_ref[0, pl.ds(k0, band), lanes]
            sc = lax.dot_general(qm, k4, (((1,), (1,)), ((), ())), preferred_element_type=F32)
            bias = jnp.concatenate(
                [bias_ref[(WIN_H - 1) - off + 2 * ii, pl.ds(gi * group, group), :]
                 for ii in range(WIN_H // 2)], axis=1)
            scores.append(sc + bias)
        return k0, scores

    def softmax_stage(scores):
        return [jnp.exp(sc - jnp.max(sc, axis=-1, keepdims=True)).astype(BF16) for sc in scores]

    def pv_stage(k0, q0, probs):
        outs = []
        for gi in range(n_groups):
            for pp in range(2):
                e2 = probs[gi][pp * pair:(pp + 1) * pair, :]
                vx = v_ref[0, pl.ds(k0, band), pl.ds((2 * gi + pp) * 2 * pair, 2 * pair)]
                pv = jnp.dot(e2, vx, preferred_element_type=F32)
                o = pv[:, :pair] / pv[:, pair:]
                outs.append(jnp.where(low_half, o[:GRID_W], o[GRID_W:]))
        att = jnp.concatenate(outs, axis=-1)
        o_ref[0, pl.ds(q0, GRID_W), :] = _rms(att, g).astype(o_ref.dtype)

    def iter_body(j, carry):
        base = j * rows_per_iter
        q0s = [pl.multiple_of((base + t) * GRID_W, GRID_W) for t in range(rows_per_iter)]
        k0s, scs, probs = {}, {}, {}
        for t in range(rows_per_iter + 2):
            if t < rows_per_iter:
                k0s[t], scs[t] = qk_stage(i * rows_per_step + base + t, q0s[t])
            if 1 <= t <= rows_per_iter:
                probs[t - 1] = softmax_stage(scs.pop(t - 1))
            if t >= 2:
                pv_stage(k0s[t - 2], q0s[t - 2], probs.pop(t - 2))
        return carry

    lax.fori_loop(0, rows_per_step // rows_per_iter, iter_body, 0)


def _attention(qk, vx, bias, g_att, rows_per_step=16, rows_per_iter=4):
    bsz, s, _ = qk.shape
    rows = s // GRID_W
    tq = rows_per_step * GRID_W
    kern = functools.partial(_attn_kernel, rows=rows, rows_per_step=rows_per_step,
                             rows_per_iter=rows_per_iter)
    return pl.pallas_call(
        kern,
        grid=(bsz, rows // rows_per_step),
        in_specs=[
            pl.BlockSpec((1, tq, D_ATT), lambda b, i: (b, i, 0)),
            pl.BlockSpec((1, s, D_ATT), lambda b, i: (b, 0, 1)),
            pl.BlockSpec((1, s, 2 * D_ATT), lambda b, i: (b, 0, 0)),
            pl.BlockSpec(bias.shape, lambda b, i: (0, 0, 0)),
            pl.BlockSpec((1, D_ATT), lambda b, i: (0, 0)),
        ],
        out_specs=pl.BlockSpec((1, tq, D_ATT), lambda b, i: (b, i, 0)),
        out_shape=jax.ShapeDtypeStruct((bsz, s, D_ATT), BF16),
        compiler_params=_params(("parallel", "arbitrary")),
        name="nbr_attention",
    )(qk, qk, vx, bias, g_att.reshape(1, D_ATT))


def _lru_au(xe_ref, a_ref, u_ref, cw_ref, cb_ref, wg_ref, bg_ref, lam_ref, *, bsz, tc, sub_t):
    c = a_ref.shape[1]
    z = -lam_ref[...]
    neg_c_softplus = -LRU_C * (jnp.maximum(z, 0.0) + jnp.log1p(jnp.exp(-jnp.abs(z))))
    sub = sub_t * bsz

    def body(sb, carry):
        r0 = pl.multiple_of(sb * sub, sub)
        xc = cb_ref[...] + cw_ref[0:1, :] * xe_ref[pl.ds(r0, sub), :]
        for j in range(1, cw_ref.shape[0]):
            xc = xc + cw_ref[j:j + 1, :] * xe_ref[pl.ds(r0 + j * bsz, sub), :]
        gates = jnp.dot(xc.astype(BF16), wg_ref[...], preferred_element_type=F32) + bg_ref[...]
        r_gate = jax.nn.sigmoid(gates[:, :c])
        i_gate = jax.nn.sigmoid(gates[:, c:])
        log_a = r_gate * neg_c_softplus
        a = jnp.exp(log_a)
        a_ref[pl.ds(r0, sub), :] = a
        one_minus_a2 = -jnp.tanh(log_a) * (a * a + 1.0)
        u_ref[pl.ds(r0, sub), :] = jnp.sqrt(one_minus_a2) * (i_gate * xc)
        return carry

    lax.fori_loop(0, tc // sub_t, body, 0)


def _fill_halo(xe_ref, xp_ref, xc_ref, xn_ref, has_prev, has_next, *, bsz, tc, pad):
    left, right = pad
    hb = xp_ref.shape[0]
    prev = xp_ref[pl.ds(hb - left * bsz, left * bsz), :]
    xe_ref[pl.ds(0, left * bsz), :] = jnp.where(has_prev, prev, 0.0)
    xe_ref[pl.ds(left * bsz, tc * bsz), :] = xc_ref[...]
    nxt = xn_ref[pl.ds(0, right * bsz), :]
    xe_ref[pl.ds((left + tc) * bsz, right * bsz), :] = jnp.where(has_next, nxt, 0.0)


def _lru_fwd_kernel(xp_ref, xc_ref, xn_ref, cw_ref, cb_ref, wg_ref, bg_ref, lam_ref,
                    hf_ref, xe_ref, a_ref, u_ref, h_ref, *, bsz, tc, sub_t, pad, unroll):
    i = pl.program_id(0)
    nc = pl.num_programs(0)

    @pl.when(i == 0)
    def _():
        h_ref[...] = jnp.zeros_like(h_ref)

    _fill_halo(xe_ref, xp_ref, xc_ref, xn_ref, i > 0, i < nc - 1, bsz=bsz, tc=tc, pad=pad)
    _lru_au(xe_ref, a_ref, u_ref, cw_ref, cb_ref, wg_ref, bg_ref, lam_ref,
            bsz=bsz, tc=tc, sub_t=sub_t)

    def step(t, h):
        rows = pl.ds(pl.multiple_of(t * bsz, bsz), bsz)
        h = a_ref[rows, :] * h + u_ref[rows, :]
        hf_ref[rows, :] = h
        return h

    h_ref[...] = lax.fori_loop(0, tc, step, h_ref[...], unroll=unroll)


def _lru_bwd_kernel(xp_ref, xc_ref, xn_ref, gate_ref, hf_ref, cw_ref, cb_ref, wg_ref, bg_ref,
                    lam_ref, grec_ref, o_ref, xe_ref, a_ref, u_ref, h_ref,
                    *, bsz, tc, sub_t, pad, unroll):
    i = pl.program_id(0)
    nc = pl.num_programs(0)

    @pl.when(i == 0)
    def _():
        h_ref[...] = jnp.zeros_like(h_ref)

    _fill_halo(xe_ref, xp_ref, xc_ref, xn_ref, i < nc - 1, i > 0, bsz=bsz, tc=tc, pad=pad)
    _lru_au(xe_ref, a_ref, u_ref, cw_ref, cb_ref, wg_ref, bg_ref, lam_ref,
            bsz=bsz, tc=tc, sub_t=sub_t)

    def step(k, h):
        t = tc - 1 - k
        rows = pl.ds(pl.multiple_of(t * bsz, bsz), bsz)
        h = a_ref[rows, :] * h + u_ref[rows, :]
        u_ref[rows, :] = h
        return h

    h_ref[...] = lax.fori_loop(0, tc, step, h_ref[...], unroll=unroll)

    sub = sub_t * bsz

    def finish(sb, carry):
        rows = pl.ds(pl.multiple_of(sb * sub, sub), sub)
        rec = (hf_ref[rows, :] + u_ref[rows, :]) * _gelu(gate_ref[rows, :])
        o_ref[rows, :] = _rms(rec, grec_ref[...]).astype(o_ref.dtype)
        return carry

    lax.fori_loop(0, tc // sub_t, finish, 0)


def _lru(rg_t, cw, cb, wg, bg, lam, g_rec, *, bsz, tc=256, sub_t=32, unroll=8):
    rows, c2 = rg_t.shape
    c = c2 // 2
    s = rows // bsz
    nc = s // tc
    k = cw.shape[0]
    pad = (k // 2, k - 1 - k // 2)
    halo_t = 8
    hb = halo_t * bsz
    n_halo = s // halo_t
    cpb = tc // halo_t
    common = dict(bsz=bsz, tc=tc, sub_t=sub_t, pad=pad, unroll=unroll)
    scratch = [
        pltpu.VMEM(((tc + k - 1) * bsz, c), F32),
        pltpu.VMEM((tc * bsz, c), F32),
        pltpu.VMEM((tc * bsz, c), F32),
        pltpu.VMEM((bsz, c), F32),
    ]

    def x_specs(chunk):
        return [
            pl.BlockSpec((hb, c), lambda i: (jnp.maximum(chunk(i) * cpb - 1, 0), 0)),
            pl.BlockSpec((tc * bsz, c), lambda i: (chunk(i), 0)),
            pl.BlockSpec((hb, c), lambda i: (jnp.minimum((chunk(i) + 1) * cpb, n_halo - 1), 0)),
        ]

    def p_specs():
        return [
            pl.BlockSpec((k, c), lambda i: (0, 0)),
            pl.BlockSpec((1, c), lambda i: (0, 0)),
            pl.BlockSpec((c, 2 * c), lambda i: (0, 0)),
            pl.BlockSpec((1, 2 * c), lambda i: (0, 0)),
            pl.BlockSpec((1, c), lambda i: (0, 0)),
        ]

    fwd = lambda i: i
    hf = pl.pallas_call(
        functools.partial(_lru_fwd_kernel, **common),
        grid=(nc,),
        in_specs=x_specs(fwd) + p_specs(),
        out_specs=pl.BlockSpec((tc * bsz, c), lambda i: (i, 0)),
        out_shape=jax.ShapeDtypeStruct((rows, c), F32),
        scratch_shapes=scratch,
        compiler_params=_params(("arbitrary",)),
        name="lru_fwd",
    )(rg_t, rg_t, rg_t, cw, cb.reshape(1, c), wg[0], bg[0].reshape(1, 2 * c), lam[0].reshape(1, c))

    bwd = lambda i: nc - 1 - i
    return pl.pallas_call(
        functools.partial(_lru_bwd_kernel, **common),
        grid=(nc,),
        in_specs=x_specs(bwd) + [
            pl.BlockSpec((tc * bsz, c), lambda i: (bwd(i), 1)),
            pl.BlockSpec((tc * bsz, c), lambda i: (bwd(i), 0)),
        ] + p_specs() + [pl.BlockSpec((1, c), lambda i: (0, 0))],
        out_specs=pl.BlockSpec((tc * bsz, c), lambda i: (bwd(i), 0)),
        out_shape=jax.ShapeDtypeStruct((rows, c), BF16),
        scratch_shapes=scratch,
        compiler_params=_params(("arbitrary",)),
        name="lru_bwd",
    )(rg_t, rg_t, rg_t, rg_t, hf, cw, cb.reshape(1, c), wg[1], bg[1].reshape(1, 2 * c),
      lam[1].reshape(1, c), g_rec.reshape(1, c))


def _gate_weights(lru_w):
    nd, ng, nb, bs, _ = lru_w.shape
    eye = jnp.eye(nb, dtype=lru_w.dtype)
    dense = jnp.einsum("xgncd,nm->xgncmd", lru_w, eye).reshape(nd, ng, nb * bs, nb * bs)
    return jnp.concatenate([dense[:, 0], dense[:, 1]], axis=-1)


def _out_up_kernel(x_ref, att_ref, rec_ref, wo_ref, g_ref, wu_ref, x1_ref, u_ref, *, n_chunks):
    mixed = jnp.concatenate([att_ref[...], rec_ref[...]], axis=-1)
    x1 = x_ref[...] + jnp.dot(mixed, wo_ref[...], preferred_element_type=F32)
    x1_ref[...] = x1
    hn = _rms(x1, g_ref[...]).astype(BF16)
    wc = u_ref.shape[1] // n_chunks
    for j in range(n_chunks):
        cols = slice(j * wc, (j + 1) * wc)
        u_ref[:, cols] = jnp.dot(hn, wu_ref[:, cols], preferred_element_type=F32).astype(u_ref.dtype)


def _out_up(x, att, rec, wo, g, wu, tm=256):
    n, d = x.shape
    c = att.shape[1]
    f2 = wu.shape[1]
    return pl.pallas_call(
        functools.partial(_out_up_kernel, n_chunks=4),
        grid=(n // tm,),
        in_specs=[
            pl.BlockSpec((tm, d), lambda i: (i, 0)),
            pl.BlockSpec((tm, c), lambda i: (i, 0)),
            pl.BlockSpec((tm, c), lambda i: (i, 0)),
            pl.BlockSpec(wo.shape, lambda i: (0, 0)),
            pl.BlockSpec((1, d), lambda i: (0, 0)),
            pl.BlockSpec(wu.shape, lambda i: (0, 0)),
        ],
        out_specs=[
            pl.BlockSpec((tm, d), lambda i: (i, 0)),
            pl.BlockSpec((tm, f2), lambda i: (i, 0)),
        ],
        out_shape=[
            jax.ShapeDtypeStruct((n, d), F32),
            jax.ShapeDtypeStruct((n, f2), BF16),
        ],
        compiler_params=_params(("parallel",)),
        name="out_up_proj",
    )(x, att, rec, wo, g.reshape(1, d), wu)


def _ffn_down_kernel(x1_ref, ua_ref, up_ref, un_ref, ul_ref, cw_ref, cb_ref, wd_ref, gf_ref, o_ref,
                     *, n_chunks, final_norm):
    j = pl.program_id(1)
    nj = pl.num_programs(1)
    tt = ua_ref.shape[1]
    hb = up_ref.shape[1]
    wc = ua_ref.shape[2] // n_chunks
    row = lax.broadcasted_iota(jnp.int32, (tt, wc), 0)
    acc = x1_ref[0]
    for ch in range(n_chunks):
        cols = pl.ds(ch * wc, wc)
        ua = ua_ref[0, :, cols].astype(F32)
        prev_row = jnp.where(j > 0, up_ref[0, hb - 1:hb, cols].astype(F32), 0.0)
        next_row = jnp.where(j < nj - 1, un_ref[0, 0:1, cols].astype(F32), 0.0)
        before = jnp.where(row == 0, prev_row, pltpu.roll(ua, 1, axis=0))
        after = jnp.where(row == tt - 1, next_row, pltpu.roll(ua, tt - 1, axis=0))
        conv = (cb_ref[:, cols] + cw_ref[0:1, cols] * before + cw_ref[1:2, cols] * ua
                + cw_ref[2:3, cols] * after)
        act = (_gelu(conv) * ul_ref[0, :, cols].astype(F32)).astype(BF16)
        acc = acc + jnp.dot(act, wd_ref[cols, :], preferred_element_type=F32)
    if final_norm:
        acc = _rms(acc, gf_ref[...])
    o_ref[0] = acc


def _ffn_down(x1, u, cw, cb, wd, g_final, *, final_norm, tt=256):
    bsz, s, d = x1.shape
    f = wd.shape[0]
    hb = 16
    tpb = tt // hb
    n_hb = s // hb
    return pl.pallas_call(
        functools.partial(_ffn_down_kernel, n_chunks=6, final_norm=final_norm),
        grid=(bsz, s // tt),
        in_specs=[
            pl.BlockSpec((1, tt, d), lambda b, j: (b, j, 0)),
            pl.BlockSpec((1, tt, f), lambda b, j: (b, j, 0)),
            pl.BlockSpec((1, hb, f), lambda b, j: (b, jnp.maximum(j * tpb - 1, 0), 0)),
            pl.BlockSpec((1, hb, f), lambda b, j: (b, jnp.minimum((j + 1) * tpb, n_hb - 1), 0)),
            pl.BlockSpec((1, tt, f), lambda b, j: (b, j, 1)),
            pl.BlockSpec(cw.shape, lambda b, j: (0, 0)),
            pl.BlockSpec((1, f), lambda b, j: (0, 0)),
            pl.BlockSpec(wd.shape, lambda b, j: (0, 0)),
            pl.BlockSpec((1, d), lambda b, j: (0, 0)),
        ],
        out_specs=pl.BlockSpec((1, tt, d), lambda b, j: (b, j, 0)),
        out_shape=jax.ShapeDtypeStruct((bsz, s, d), F32),
        compiler_params=_params(("parallel", "arbitrary")),
        name="ffn_down",
    )(x1, u, u, u, u, cw, cb.reshape(1, f), wd, g_final.reshape(1, d))


def kernel(x, g_mix, w_in, rel_bias, conv_lru_w, conv_lru_b, lru_w, lru_b, lru_lam, g_att, g_rec,
           w_out, g_ffn, w_up, conv_ffn_w, conv_ffn_b, w_down, g_final):
    bsz, s, d = x.shape
    depth = w_in.shape[0]
    n = bsz * s
    c_lru = conv_lru_w.shape[-1]
    w_in_b = w_in.astype(BF16)
    w_out_b = w_out.astype(BF16)
    w_up_b = w_up.astype(BF16)
    w_down_b = w_down.astype(BF16)
    for l in range(depth):
        qk, vx, rg = _in_proj(x.reshape(n, d), g_mix[l], w_in_b[l])
        att = _attention(qk.reshape(bsz, s, 2 * D_ATT), vx.reshape(bsz, s, 2 * D_ATT),
                         _bias_pieces(rel_bias[l]), g_att[l])
        rg_t = jnp.transpose(rg.reshape(bsz, s, 2 * c_lru), (1, 0, 2)).reshape(s * bsz, 2 * c_lru)
        wg = _gate_weights(lru_w[l]).astype(BF16)
        bg = jnp.concatenate([lru_b[l, :, 0], lru_b[l, :, 1]], axis=-1)
        rec_t = _lru(rg_t, conv_lru_w[l], conv_lru_b[l], wg, bg, lru_lam[l], g_rec[l], bsz=bsz)
        rec = jnp.transpose(rec_t.reshape(s, bsz, c_lru), (1, 0, 2)).reshape(n, c_lru)
        x1, u = _out_up(x.reshape(n, d), att.reshape(n, D_ATT), rec, w_out_b[l], g_ffn[l], w_up_b[l])
        x = _ffn_down(x1.reshape(bsz, s, d), u.reshape(bsz, s, -1), conv_ffn_w[l], conv_ffn_b[l],
                      w_down_b[l], g_final, final_norm=(l == depth - 1))
    return x
```

```python
import functools
import math

import jax
import jax.numpy as jnp
from jax import lax
from jax.experimental import pallas as pl
from jax.experimental.pallas import tpu as pltpu

F32 = jnp.float32
BF16 = jnp.bfloat16

GRID_W = 64
N_HEADS = 8
HEAD_DIM = 64
D_ATT = N_HEADS * HEAD_DIM
WIN_H = 8
WIN_W = 16
LRU_C = 8.0
EPS = 1e-6
MASKED = -1e30
VMEM_LIMIT = 56 * 1024 * 1024


def _rms(x, g):
    ms = jnp.mean(x * x, axis=-1, keepdims=True)
    return x * lax.rsqrt(ms + EPS) * g


def _gelu(x):
    c = math.sqrt(2.0 / math.pi)
    return 0.5 * x * (1.0 + jnp.tanh(c * (x + 0.044715 * (x * x * x))))


def _sigmoid(x):
    return 0.5 + 0.5 * jnp.tanh(0.5 * x)


def _params(sem):
    return pltpu.CompilerParams(dimension_semantics=sem, vmem_limit_bytes=VMEM_LIMIT)


def _norm_cast_kernel(x_ref, g_ref, o_ref):
    o_ref[...] = _rms(x_ref[...], g_ref[...]).astype(o_ref.dtype)


def _norm_cast(x, g, tm=512):
    n, d = x.shape
    return pl.pallas_call(
        _norm_cast_kernel,
        grid=(n // tm,),
        in_specs=[pl.BlockSpec((tm, d), lambda i: (i, 0)), pl.BlockSpec((1, d), lambda i: (0, 0))],
        out_specs=pl.BlockSpec((tm, d), lambda i: (i, 0)),
        out_shape=jax.ShapeDtypeStruct((n, d), BF16),
        compiler_params=_params(("parallel",)),
        name="norm_cast",
    )(x, g.reshape(1, d))


def _in_proj_kernel(h_ref, w_ref, qk_ref, vx_ref, rg_ref):
    hn = h_ref[...]
    pair = 2 * HEAD_DIM
    q = jnp.dot(hn, w_ref[:, :D_ATT], preferred_element_type=F32)
    qk_ref[:, :D_ATT] = (q * (HEAD_DIM ** -0.5)).astype(BF16)
    k = jnp.dot(hn, w_ref[:, D_ATT:2 * D_ATT], preferred_element_type=F32)
    qk_ref[:, D_ATT:] = k.astype(BF16)
    v = jnp.dot(hn, w_ref[:, 2 * D_ATT:3 * D_ATT], preferred_element_type=F32).astype(BF16)
    ones = jnp.ones((v.shape[0], pair), BF16)
    for p in range(D_ATT // pair):
        vx_ref[:, 2 * p * pair:(2 * p + 1) * pair] = v[:, p * pair:(p + 1) * pair]
        vx_ref[:, (2 * p + 1) * pair:(2 * p + 2) * pair] = ones
    rg_ref[...] = jnp.dot(hn, w_ref[:, 3 * D_ATT:], preferred_element_type=F32)


def _in_proj(hn, w, tm=512):
    n, d = hn.shape
    d_in = w.shape[1]
    d_rg = d_in - 3 * D_ATT
    return pl.pallas_call(
        _in_proj_kernel,
        grid=(n // tm,),
        in_specs=[
            pl.BlockSpec((tm, d), lambda i: (i, 0)),
            pl.BlockSpec((d, d_in), lambda i: (0, 0)),
        ],
        out_specs=[
            pl.BlockSpec((tm, 2 * D_ATT), lambda i: (i, 0)),
            pl.BlockSpec((tm, 2 * D_ATT), lambda i: (i, 0)),
            pl.BlockSpec((tm, d_rg), lambda i: (i, 0)),
        ],
        out_shape=[
            jax.ShapeDtypeStruct((n, 2 * D_ATT), BF16),
            jax.ShapeDtypeStruct((n, 2 * D_ATT), BF16),
            jax.ShapeDtypeStruct((n, d_rg), F32),
        ],
        compiler_params=_params(("parallel",)),
        name="in_proj",
    )(hn, w)


def _bias_rows(rel_bias):
    h = rel_bias.shape[0]
    z = jnp.zeros((h, 2 * WIN_H - 2, 33), rel_bias.dtype)
    return jnp.concatenate(
        [rel_bias[:, :-1, WIN_W - 1:], z, rel_bias[:, 1:, :], z, rel_bias[:, :-1, :WIN_W - 1]], axis=-1)


def _bias_kernel(rows_ref, o_ref):
    n_dr = o_ref.shape[0]
    shape = (GRID_W, 2 * GRID_W)
    c = lax.broadcasted_iota(jnp.int32, shape, 0)
    kc = lax.broadcasted_iota(jnp.int32, shape, 1) & (GRID_W - 1)
    col_start = jnp.clip(c - WIN_W // 2, 0, GRID_W - WIN_W)
    valid = (kc >= col_start) & (kc < col_start + WIN_W)
    for h in range(N_HEADS):
        for dr in range(n_dr):
            row = jnp.broadcast_to(rows_ref[h, dr:dr + 1, :], shape)
            toeplitz = pltpu.roll(row, 0, 1, stride=1, stride_axis=0)
            o_ref[dr, h * GRID_W:(h + 1) * GRID_W, :] = jnp.where(valid, toeplitz, MASKED)


def _bias_pieces(rel_bias):
    rows = _bias_rows(rel_bias.astype(F32))
    n_dr = rows.shape[1]
    return pl.pallas_call(
        _bias_kernel,
        out_shape=jax.ShapeDtypeStruct((n_dr, N_HEADS * GRID_W, 2 * GRID_W), F32),
        name="bias_pieces",
    )(rows)


def _attn_kernel(q_ref, k_ref, v_ref, bias_ref, g_ref, o_ref, *, rows, rows_per_step, rows_per_iter):
    i = pl.program_id(1)
    band = WIN_H * GRID_W
    pair = 2 * HEAD_DIM
    group = 4 * HEAD_DIM
    n_groups = D_ATT // group
    lane = lax.broadcasted_iota(jnp.int32, (GRID_W, group), 1)
    head_lanes = [(lane >= h * HEAD_DIM) & (lane < (h + 1) * HEAD_DIM) for h in range(4)]
    low_half = lax.broadcasted_iota(jnp.int32, (GRID_W, pair), 1) < HEAD_DIM
    g = g_ref[...]

    def qk_stage(r, q0):
        row_start = jnp.clip(r - WIN_H // 2, 0, rows - WIN_H)
        off = r - row_start
        k0 = pl.multiple_of(row_start * GRID_W, GRID_W)
        scores = []
        for gi in range(n_groups):
            lanes = pl.ds(gi * group, group)
            q4 = q_ref[0, pl.ds(q0, GRID_W), lanes]
            qm = jnp.concatenate([jnp.where(m, q4, jnp.zeros_like(q4)) for m in head_lanes], axis=0)
            k4 = k_ref[0, pl.ds(k0, band), lanes]
            sc = lax.dot_general(qm, k4, (((1,), (1,)), ((), ())), preferred_element_type=F32)
            bias = jnp.concatenate(
                [bias_ref[(WIN_H - 1) - off + 2 * ii, pl.ds(gi * group, group), :]
                 for ii in range(WIN_H // 2)], axis=1)
            scores.append(sc + bias)
        return k0, scores

    def softmax_stage(scores):
        return [jnp.exp(sc - jnp.max(sc, axis=-1, keepdims=True)).astype(BF16) for sc in scores]

    def pv_stage(k0, q0, probs):
        outs = []
        for gi in range(n_groups):
            for pp in range(2):
                e2 = probs[gi][pp * pair:(pp + 1) * pair, :]
                vx = v_ref[0, pl.ds(k0, band), pl.ds((2 * gi + pp) * 2 * pair, 2 * pair)]
                pv = jnp.dot(e2, vx, preferred_element_type=F32)
                o = pv[:, :pair] / pv[:, pair:]
                outs.append(jnp.where(low_half, o[:GRID_W], o[GRID_W:]))
        att = jnp.concatenate(outs, axis=-1)
        o_ref[0, pl.ds(q0, GRID_W), :] = _rms(att, g).astype(o_ref.dtype)

    def iter_body(j, carry):
        base = j * rows_per_iter
        q0s = [pl.multiple_of((base + t) * GRID_W, GRID_W) for t in range(rows_per_iter)]
        k0s, scs, probs = {}, {}, {}
        for t in range(rows_per_iter + 2):
            if t < rows_per_iter:
                k0s[t], scs[t] = qk_stage(i * rows_per_step + base + t, q0s[t])
            if 1 <= t <= rows_per_iter:
                probs[t - 1] = softmax_stage(scs.pop(t - 1))
            if t >= 2:
                pv_stage(k0s[t - 2], q0s[t - 2], probs.pop(t - 2))
        return carry

    lax.fori_loop(0, rows_per_step // rows_per_iter, iter_body, 0)


def _attention(qk, vx, bias, g_att, rows_per_step=16, rows_per_iter=8):
    bsz, s, _ = qk.shape
    rows = s // GRID_W
    tq = rows_per_step * GRID_W
    kern = functools.partial(_attn_kernel, rows=rows, rows_per_step=rows_per_step,
                             rows_per_iter=rows_per_iter)
    return pl.pallas_call(
        kern,
        grid=(bsz, rows // rows_per_step),
        in_specs=[
            pl.BlockSpec((1, tq, D_ATT), lambda b, i: (b, i, 0)),
            pl.BlockSpec((1, s, D_ATT), lambda b, i: (b, 0, 1)),
            pl.BlockSpec((1, s, 2 * D_ATT), lambda b, i: (b, 0, 0)),
            pl.BlockSpec(bias.shape, lambda b, i: (0, 0, 0)),
            pl.BlockSpec((1, D_ATT), lambda b, i: (0, 0)),
        ],
        out_specs=pl.BlockSpec((1, tq, D_ATT), lambda b, i: (b, i, 0)),
        out_shape=jax.ShapeDtypeStruct((bsz, s, D_ATT), BF16),
        compiler_params=_params(("parallel", "arbitrary")),
        name="nbr_attention",
    )(qk, qk, vx, bias, g_att.reshape(1, D_ATT))


def _lru_au(xe_ref, a_ref, u_ref, cw_ref, cb_ref, wg_ref, bg_ref, lam_ref, *, bsz, tc, sub_t):
    c = a_ref.shape[1]
    z = -lam_ref[...]
    neg_c_softplus = -LRU_C * (jnp.maximum(z, 0.0) + jnp.log1p(jnp.exp(-jnp.abs(z))))
    sub = sub_t * bsz

    def body(sb, carry):
        r0 = pl.multiple_of(sb * sub, sub)
        xc = cb_ref[...] + cw_ref[0:1, :] * xe_ref[pl.ds(r0, sub), :]
        for j in range(1, cw_ref.shape[0]):
            xc = xc + cw_ref[j:j + 1, :] * xe_ref[pl.ds(r0 + j * bsz, sub), :]
        gates = jnp.dot(xc.astype(BF16), wg_ref[...], preferred_element_type=F32) + bg_ref[...]
        r_gate = _sigmoid(gates[:, :c])
        i_gate = _sigmoid(gates[:, c:])
        log_a = r_gate * neg_c_softplus
        a = jnp.exp(log_a)
        a_ref[pl.ds(r0, sub), :] = a
        one_minus_a2 = -jnp.tanh(log_a) * (a * a + 1.0)
        u_ref[pl.ds(r0, sub), :] = jnp.sqrt(one_minus_a2) * (i_gate * xc)
        return carry

    lax.fori_loop(0, tc // sub_t, body, 0)


def _fill_halo(xe_ref, xp_ref, xc_ref, xn_ref, has_prev, has_next, *, bsz, tc, pad):
    left, right = pad
    hb = xp_ref.shape[0]
    prev = xp_ref[pl.ds(hb - left * bsz, left * bsz), :]
    xe_ref[pl.ds(0, left * bsz), :] = jnp.where(has_prev, prev, 0.0)
    xe_ref[pl.ds(left * bsz, tc * bsz), :] = xc_ref[...]
    nxt = xn_ref[pl.ds(0, right * bsz), :]
    xe_ref[pl.ds((left + tc) * bsz, right * bsz), :] = jnp.where(has_next, nxt, 0.0)


def _lru_fwd_kernel(xp_ref, xc_ref, xn_ref, cw_ref, cb_ref, wg_ref, bg_ref, lam_ref,
                    hf_ref, xe_ref, a_ref, u_ref, h_ref, *, bsz, tc, sub_t, pad, unroll):
    i = pl.program_id(0)
    nc = pl.num_programs(0)

    @pl.when(i == 0)
    def _():
        h_ref[...] = jnp.zeros_like(h_ref)

    _fill_halo(xe_ref, xp_ref, xc_ref, xn_ref, i > 0, i < nc - 1, bsz=bsz, tc=tc, pad=pad)
    _lru_au(xe_ref, a_ref, u_ref, cw_ref, cb_ref, wg_ref, bg_ref, lam_ref,
            bsz=bsz, tc=tc, sub_t=sub_t)

    def step(t, h):
        rows = pl.ds(pl.multiple_of(t * bsz, bsz), bsz)
        h = a_ref[rows, :] * h + u_ref[rows, :]
        hf_ref[rows, :] = h
        return h

    h_ref[...] = lax.fori_loop(0, tc, step, h_ref[...], unroll=unroll)


def _lru_bwd_kernel(xp_ref, xc_ref, xn_ref, gate_ref, hf_ref, cw_ref, cb_ref, wg_ref, bg_ref,
                    lam_ref, grec_ref, o_ref, xe_ref, a_ref, u_ref, h_ref,
                    *, bsz, tc, sub_t, pad, unroll):
    i = pl.program_id(0)
    nc = pl.num_programs(0)

    @pl.when(i == 0)
    def _():
        h_ref[...] = jnp.zeros_like(h_ref)

    _fill_halo(xe_ref, xp_ref, xc_ref, xn_ref, i < nc - 1, i > 0, bsz=bsz, tc=tc, pad=pad)
    _lru_au(xe_ref, a_ref, u_ref, cw_ref, cb_ref, wg_ref, bg_ref, lam_ref,
            bsz=bsz, tc=tc, sub_t=sub_t)

    def step(k, h):
        t = tc - 1 - k
        rows = pl.ds(pl.multiple_of(t * bsz, bsz), bsz)
        h = a_ref[rows, :] * h + u_ref[rows, :]
        u_ref[rows, :] = h
        return h

    h_ref[...] = lax.fori_loop(0, tc, step, h_ref[...], unroll=unroll)

    sub = sub_t * bsz

    def finish(sb, carry):
        rows = pl.ds(pl.multiple_of(sb * sub, sub), sub)
        rec = (hf_ref[rows, :] + u_ref[rows, :]) * _gelu(gate_ref[rows, :])
        o_ref[rows, :] = _rms(rec, grec_ref[...]).astype(o_ref.dtype)
        return carry

    lax.fori_loop(0, tc // sub_t, finish, 0)


def _lru(rg_t, cw, cb, wg, bg, lam, g_rec, *, bsz, tc=256, sub_t=32, unroll=8):
    rows, c2 = rg_t.shape
    c = c2 // 2
    s = rows // bsz
    nc = s // tc
    k = cw.shape[0]
    pad = (k // 2, k - 1 - k // 2)
    halo_t = 8
    hb = halo_t * bsz
    n_halo = s // halo_t
    cpb = tc // halo_t
    common = dict(bsz=bsz, tc=tc, sub_t=sub_t, pad=pad, unroll=unroll)
    scratch = [
        pltpu.VMEM(((tc + k - 1) * bsz, c), F32),
        pltpu.VMEM((tc * bsz, c), F32),
        pltpu.VMEM((tc * bsz, c), F32),
        pltpu.VMEM((bsz, c), F32),
    ]

    def x_specs(chunk):
        return [
            pl.BlockSpec((hb, c), lambda i: (jnp.maximum(chunk(i) * cpb - 1, 0), 0)),
            pl.BlockSpec((tc * bsz, c), lambda i: (chunk(i), 0)),
            pl.BlockSpec((hb, c), lambda i: (jnp.minimum((chunk(i) + 1) * cpb, n_halo - 1), 0)),
        ]

    def p_specs():
        return [
            pl.BlockSpec((k, c), lambda i: (0, 0)),
            pl.BlockSpec((1, c), lambda i: (0, 0)),
            pl.BlockSpec((c, 2 * c), lambda i: (0, 0)),
            pl.BlockSpec((1, 2 * c), lambda i: (0, 0)),
            pl.BlockSpec((1, c), lambda i: (0, 0)),
        ]

    fwd = lambda i: i
    hf = pl.pallas_call(
        functools.partial(_lru_fwd_kernel, **common),
        grid=(nc,),
        in_specs=x_specs(fwd) + p_specs(),
        out_specs=pl.BlockSpec((tc * bsz, c), lambda i: (i, 0)),
        out_shape=jax.ShapeDtypeStruct((rows, c), F32),
        scratch_shapes=scratch,
        compiler_params=_params(("arbitrary",)),
        name="lru_fwd",
    )(rg_t, rg_t, rg_t, cw, cb.reshape(1, c), wg[0], bg[0].reshape(1, 2 * c), lam[0].reshape(1, c))

    bwd = lambda i: nc - 1 - i
    return pl.pallas_call(
        functools.partial(_lru_bwd_kernel, **common),
        grid=(nc,),
        in_specs=x_specs(bwd) + [
            pl.BlockSpec((tc * bsz, c), lambda i: (bwd(i), 1)),
            pl.BlockSpec((tc * bsz, c), lambda i: (bwd(i), 0)),
        ] + p_specs() + [pl.BlockSpec((1, c), lambda i: (0, 0))],
        out_specs=pl.BlockSpec((tc * bsz, c), lambda i: (bwd(i), 0)),
        out_shape=jax.ShapeDtypeStruct((rows, c), BF16),
        scratch_shapes=scratch,
        compiler_params=_params(("arbitrary",)),
        name="lru_bwd",
    )(rg_t, rg_t, rg_t, rg_t, hf, cw, cb.reshape(1, c), wg[1], bg[1].reshape(1, 2 * c),
      lam[1].reshape(1, c), g_rec.reshape(1, c))


def _gate_weights(lru_w):
    nd, ng, nb, bs, _ = lru_w.shape
    eye = jnp.eye(nb, dtype=lru_w.dtype)
    dense = jnp.einsum("xgncd,nm->xgncmd", lru_w, eye).reshape(nd, ng, nb * bs, nb * bs)
    return jnp.concatenate([dense[:, 0], dense[:, 1]], axis=-1)


def _out_proj_kernel(x_ref, att_ref, rec_ref, wo_ref, g_ref, x1_ref, hn_ref):
    mixed = jnp.concatenate([att_ref[...], rec_ref[...]], axis=-1)
    x1 = x_ref[...] + jnp.dot(mixed, wo_ref[...], preferred_element_type=F32)
    x1_ref[...] = x1
    hn_ref[...] = _rms(x1, g_ref[...]).astype(hn_ref.dtype)


def _out_proj(x, att, rec, wo, g, tm=512):
    n, d = x.shape
    c = att.shape[1]
    return pl.pallas_call(
        _out_proj_kernel,
        grid=(n // tm,),
        in_specs=[
            pl.BlockSpec((tm, d), lambda i: (i, 0)),
            pl.BlockSpec((tm, c), lambda i: (i, 0)),
            pl.BlockSpec((tm, c), lambda i: (i, 0)),
            pl.BlockSpec(wo.shape, lambda i: (0, 0)),
            pl.BlockSpec((1, d), lambda i: (0, 0)),
        ],
        out_specs=[
            pl.BlockSpec((tm, d), lambda i: (i, 0)),
            pl.BlockSpec((tm, d), lambda i: (i, 0)),
        ],
        out_shape=[
            jax.ShapeDtypeStruct((n, d), F32),
            jax.ShapeDtypeStruct((n, d), BF16),
        ],
        compiler_params=_params(("parallel",)),
        name="out_proj",
    )(x, att, rec, wo, g.reshape(1, d))


HALO = 16


def _ffn_kernel(hc_ref, hp_ref, hx_ref, x1_ref, wu_ref, cw_ref, cb_ref, wd_ref, g_ref, *out_refs,
                n_chunks, last):
    j = pl.program_id(1)
    nj = pl.num_programs(1)
    tm = hc_ref.shape[1]
    f = wd_ref.shape[0]
    wc = f // n_chunks
    hn = hc_ref[0]
    prev = jnp.where(j > 0, hp_ref[0], jnp.zeros_like(hp_ref[0]))
    nxt = jnp.where(j < nj - 1, hx_ref[0], jnp.zeros_like(hx_ref[0]))
    hn_ext = jnp.concatenate([prev, hn, nxt], axis=0)
    ext = tm + 2 * HALO
    gelu_c = math.sqrt(2.0 / math.pi)
    acc = x1_ref[0]

    def up_proj(ch):
        return (jnp.dot(hn_ext, wu_ref[:, pl.ds(ch * wc, wc)], preferred_element_type=F32),
                jnp.dot(hn, wu_ref[:, pl.ds(f + ch * wc, wc)], preferred_element_type=F32))

    pending = up_proj(0)
    for ch in range(n_chunks):
        cols = pl.ds(ch * wc, wc)
        ua_ext, ul = pending
        if ch + 1 < n_chunks:
            pending = up_proj(ch + 1)
        before = pltpu.roll(ua_ext, 1, axis=0)[HALO:HALO + tm]
        after = pltpu.roll(ua_ext, ext - 1, axis=0)[HALO:HALO + tm]
        conv = (cb_ref[:, cols] + cw_ref[0:1, cols] * before + cw_ref[1:2, cols] * ua_ext[HALO:HALO + tm]
                + cw_ref[2:3, cols] * after)
        inner = conv * (conv * conv * (gelu_c * 0.044715) + gelu_c)
        z = (0.5 * conv) * ul
        act = (z + z * jnp.tanh(inner)).astype(BF16)
        acc = acc + jnp.dot(act, wd_ref[cols, :], preferred_element_type=F32)
    if last:
        out_refs[0][0] = _rms(acc, g_ref[...])
    else:
        out_refs[0][0] = acc
        out_refs[1][0] = _rms(acc, g_ref[...]).astype(out_refs[1].dtype)


def _ffn(hn, x1, wu, cw, cb, wd, g_next, *, last, tm=256):
    bsz, s, d = x1.shape
    f = wd.shape[0]
    tpb = tm // HALO
    n_hb = s // HALO
    tile = pl.BlockSpec((1, tm, d), lambda b, j: (b, j, 0))
    if last:
        out_specs = [tile]
        out_shape = [jax.ShapeDtypeStruct((bsz, s, d), F32)]
    else:
        out_specs = [tile, tile]
        out_shape = [jax.ShapeDtypeStruct((bsz, s, d), F32), jax.ShapeDtypeStruct((bsz, s, d), BF16)]
    return pl.pallas_call(
        functools.partial(_ffn_kernel, n_chunks=6, last=last),
        grid=(bsz, s // tm),
        in_specs=[
            tile,
            pl.BlockSpec((1, HALO, d), lambda b, j: (b, jnp.maximum(j * tpb - 1, 0), 0)),
            pl.BlockSpec((1, HALO, d), lambda b, j: (b, jnp.minimum((j + 1) * tpb, n_hb - 1), 0)),
            tile,
            pl.BlockSpec(wu.shape, lambda b, j: (0, 0)),
            pl.BlockSpec(cw.shape, lambda b, j: (0, 0)),
            pl.BlockSpec((1, f), lambda b, j: (0, 0)),
            pl.BlockSpec(wd.shape, lambda b, j: (0, 0)),
            pl.BlockSpec((1, d), lambda b, j: (0, 0)),
        ],
        out_specs=out_specs,
        out_shape=out_shape,
        compiler_params=_params(("parallel", "arbitrary")),
        name="ffn",
    )(hn, hn, hn, x1, wu, cw, cb.reshape(1, f), wd, g_next.reshape(1, d))


def kernel(x, g_mix, w_in, rel_bias, conv_lru_w, conv_lru_b, lru_w, lru_b, lru_lam, g_att, g_rec,
           w_out, g_ffn, w_up, conv_ffn_w, conv_ffn_b, w_down, g_final):
    bsz, s, d = x.shape
    depth = w_in.shape[0]
    n = bsz * s
    c_lru = conv_lru_w.shape[-1]
    w_in_b = w_in.astype(BF16)
    w_out_b = w_out.astype(BF16)
    w_up_b = w_up.astype(BF16)
    w_down_b = w_down.astype(BF16)
    hn = _norm_cast(x.reshape(n, d), g_mix[0])
    for l in range(depth):
        last = l == depth - 1
        qk, vx, rg = _in_proj(hn, w_in_b[l])
        att = _attention(qk.reshape(bsz, s, 2 * D_ATT), vx.reshape(bsz, s, 2 * D_ATT),
                         _bias_pieces(rel_bias[l]), g_att[l])
        rg_t = jnp.transpose(rg.reshape(bsz, s, 2 * c_lru), (1, 0, 2)).reshape(s * bsz, 2 * c_lru)
        wg = _gate_weights(lru_w[l]).astype(BF16)
        bg = jnp.concatenate([lru_b[l, :, 0], lru_b[l, :, 1]], axis=-1)
        rec_t = _lru(rg_t, conv_lru_w[l], conv_lru_b[l], wg, bg, lru_lam[l], g_rec[l], bsz=bsz)
        rec = jnp.transpose(rec_t.reshape(s, bsz, c_lru), (1, 0, 2)).reshape(n, c_lru)
        x1, hf = _out_proj(x.reshape(n, d), att.reshape(n, D_ATT), rec, w_out_b[l], g_ffn[l])
        outs = _ffn(hf.reshape(bsz, s, d), x1.reshape(bsz, s, d), w_up_b[l], conv_ffn_w[l],
                    conv_ffn_b[l], w_down_b[l], g_final if last else g_mix[l + 1], last=last)
        if last:
            return outs[0]
        x, hn = outs[0], outs[1].reshape(n, d)
```

```python
import functools
import math

import jax
import jax.numpy as jnp
from jax import lax
from jax.experimental import pallas as pl
from jax.experimental.pallas import tpu as pltpu

F32 = jnp.float32
BF16 = jnp.bfloat16

GRID_W = 64
N_HEADS = 8
HEAD_DIM = 64
D_ATT = N_HEADS * HEAD_DIM
WIN_H = 8
WIN_W = 16
LRU_C = 8.0
EPS = 1e-6
MASKED = -1e30
VMEM_LIMIT = 56 * 1024 * 1024


def _rms(x, g):
    ms = jnp.mean(x * x, axis=-1, keepdims=True)
    return x * lax.rsqrt(ms + EPS) * g


def _gelu(x):
    c = math.sqrt(2.0 / math.pi)
    return 0.5 * x * (1.0 + jnp.tanh(c * (x + 0.044715 * (x * x * x))))


def _sigmoid(x):
    return 0.5 + 0.5 * jnp.tanh(0.5 * x)


def _params(sem):
    return pltpu.CompilerParams(dimension_semantics=sem, vmem_limit_bytes=VMEM_LIMIT)


def _norm_cast_kernel(x_ref, g_ref, o_ref):
    o_ref[...] = _rms(x_ref[...], g_ref[...]).astype(o_ref.dtype)


def _norm_cast(x, g, tm=512):
    n, d = x.shape
    return pl.pallas_call(
        _norm_cast_kernel,
        grid=(n // tm,),
        in_specs=[pl.BlockSpec((tm, d), lambda i: (i, 0)), pl.BlockSpec((1, d), lambda i: (0, 0))],
        out_specs=pl.BlockSpec((tm, d), lambda i: (i, 0)),
        out_shape=jax.ShapeDtypeStruct((n, d), BF16),
        compiler_params=_params(("parallel",)),
        name="norm_cast",
    )(x, g.reshape(1, d))


def _row_permutation(n_outer, n_inner):
    r = jnp.arange(n_outer * n_inner)
    src = (r % n_inner) * n_outer + r // n_inner
    return jax.nn.one_hot(src, n_outer * n_inner, dtype=BF16)


def _in_proj_kernel(h_ref, p_ref, w_ref, qk_ref, vx_ref, rg_ref):
    nb, tt, d = h_ref.shape
    hn = h_ref[...].reshape(nb * tt, d)
    pair = 2 * HEAD_DIM
    q = jnp.dot(hn, w_ref[:, :D_ATT], preferred_element_type=F32)
    qk_ref[:, :, :D_ATT] = (q * (HEAD_DIM ** -0.5)).astype(BF16).reshape(nb, tt, D_ATT)
    k = jnp.dot(hn, w_ref[:, D_ATT:2 * D_ATT], preferred_element_type=F32)
    qk_ref[:, :, D_ATT:] = k.astype(BF16).reshape(nb, tt, D_ATT)
    v = jnp.dot(hn, w_ref[:, 2 * D_ATT:3 * D_ATT], preferred_element_type=F32).astype(BF16)
    ones = jnp.ones((nb, tt, pair), BF16)
    for p in range(D_ATT // pair):
        vx_ref[:, :, 2 * p * pair:(2 * p + 1) * pair] = v[:, p * pair:(p + 1) * pair].reshape(nb, tt, pair)
        vx_ref[:, :, (2 * p + 1) * pair:(2 * p + 2) * pair] = ones
    hn_t = jnp.dot(p_ref[...], hn, preferred_element_type=F32).astype(BF16)
    rg_ref[...] = jnp.dot(hn_t, w_ref[:, 3 * D_ATT:], preferred_element_type=F32)


def _in_proj(hn, w, tt=64):
    bsz, s, d = hn.shape
    d_in = w.shape[1]
    d_rg = d_in - 3 * D_ATT
    perm = _row_permutation(tt, bsz)
    bm_tile = pl.BlockSpec((bsz, tt, 2 * D_ATT), lambda i: (0, i, 0))
    return pl.pallas_call(
        _in_proj_kernel,
        grid=(s // tt,),
        in_specs=[
            pl.BlockSpec((bsz, tt, d), lambda i: (0, i, 0)),
            pl.BlockSpec(perm.shape, lambda i: (0, 0)),
            pl.BlockSpec((d, d_in), lambda i: (0, 0)),
        ],
        out_specs=[bm_tile, bm_tile, pl.BlockSpec((tt * bsz, d_rg), lambda i: (i, 0))],
        out_shape=[
            jax.ShapeDtypeStruct((bsz, s, 2 * D_ATT), BF16),
            jax.ShapeDtypeStruct((bsz, s, 2 * D_ATT), BF16),
            jax.ShapeDtypeStruct((s * bsz, d_rg), F32),
        ],
        compiler_params=_params(("parallel",)),
        name="in_proj",
    )(hn, perm, w)


def _bias_rows(rel_bias):
    h = rel_bias.shape[0]
    z = jnp.zeros((h, 2 * WIN_H - 2, 33), rel_bias.dtype)
    return jnp.concatenate(
        [rel_bias[:, :-1, WIN_W - 1:], z, rel_bias[:, 1:, :], z, rel_bias[:, :-1, :WIN_W - 1]], axis=-1)


def _bias_kernel(rows_ref, o_ref):
    n_dr = o_ref.shape[0]
    shape = (GRID_W, 2 * GRID_W)
    c = lax.broadcasted_iota(jnp.int32, shape, 0)
    kc = lax.broadcasted_iota(jnp.int32, shape, 1) & (GRID_W - 1)
    col_start = jnp.clip(c - WIN_W // 2, 0, GRID_W - WIN_W)
    valid = (kc >= col_start) & (kc < col_start + WIN_W)
    for h in range(N_HEADS):
        for dr in range(n_dr):
            row = jnp.broadcast_to(rows_ref[h, dr:dr + 1, :], shape)
            toeplitz = pltpu.roll(row, 0, 1, stride=1, stride_axis=0)
            o_ref[dr, h * GRID_W:(h + 1) * GRID_W, :] = jnp.where(valid, toeplitz, MASKED)


def _bias_pieces(rel_bias):
    rows = _bias_rows(rel_bias.astype(F32))
    n_dr = rows.shape[1]
    return pl.pallas_call(
        _bias_kernel,
        out_shape=jax.ShapeDtypeStruct((n_dr, N_HEADS * GRID_W, 2 * GRID_W), F32),
        name="bias_pieces",
    )(rows)


def _attn_kernel(q_ref, k_ref, v_ref, bias_ref, g_ref, o_ref, *, rows, rows_per_step, rows_per_iter):
    i = pl.program_id(1)
    band = WIN_H * GRID_W
    pair = 2 * HEAD_DIM
    group = 4 * HEAD_DIM
    n_groups = D_ATT // group
    lane = lax.broadcasted_iota(jnp.int32, (GRID_W, group), 1)
    head_lanes = [(lane >= h * HEAD_DIM) & (lane < (h + 1) * HEAD_DIM) for h in range(4)]
    low_half = lax.broadcasted_iota(jnp.int32, (GRID_W, pair), 1) < HEAD_DIM
    g = g_ref[...]

    def qk_stage(r, q0):
        row_start = jnp.clip(r - WIN_H // 2, 0, rows - WIN_H)
        off = r - row_start
        k0 = pl.multiple_of(row_start * GRID_W, GRID_W)
        scores = []
        for gi in range(n_groups):
            lanes = pl.ds(gi * group, group)
            q4 = q_ref[0, pl.ds(q0, GRID_W), lanes]
            qm = jnp.concatenate([jnp.where(m, q4, jnp.zeros_like(q4)) for m in head_lanes], axis=0)
            k4 = k_ref[0, pl.ds(k0, band), lanes]
            sc = lax.dot_general(qm, k4, (((1,), (1,)), ((), ())), preferred_element_type=F32)
            bias = jnp.concatenate(
                [bias_ref[(WIN_H - 1) - off + 2 * ii, pl.ds(gi * group, group), :]
                 for ii in range(WIN_H // 2)], axis=1)
            scores.append(sc + bias)
        return k0, scores

    def softmax_stage(scores):
        return [jnp.exp(sc - jnp.max(sc, axis=-1, keepdims=True)).astype(BF16) for sc in scores]

    def pv_stage(k0, q0, probs):
        outs = []
        for gi in range(n_groups):
            for pp in range(2):
                e2 = probs[gi][pp * pair:(pp + 1) * pair, :]
                vx = v_ref[0, pl.ds(k0, band), pl.ds((2 * gi + pp) * 2 * pair, 2 * pair)]
                pv = jnp.dot(e2, vx, preferred_element_type=F32)
                o = pv[:, :pair] / pv[:, pair:]
                outs.append(jnp.where(low_half, o[:GRID_W], o[GRID_W:]))
        att = jnp.concatenate(outs, axis=-1)
        o_ref[0, pl.ds(q0, GRID_W), :] = _rms(att, g).astype(o_ref.dtype)

    def iter_body(j, carry):
        base = j * rows_per_iter
        q0s = [pl.multiple_of((base + t) * GRID_W, GRID_W) for t in range(rows_per_iter)]
        k0s, scs, probs = {}, {}, {}
        for t in range(rows_per_iter + 2):
            if t < rows_per_iter:
                k0s[t], scs[t] = qk_stage(i * rows_per_step + base + t, q0s[t])
            if 1 <= t <= rows_per_iter:
                probs[t - 1] = softmax_stage(scs.pop(t - 1))
            if t >= 2:
                pv_stage(k0s[t - 2], q0s[t - 2], probs.pop(t - 2))
        return carry

    lax.fori_loop(0, rows_per_step // rows_per_iter, iter_body, 0)


def _attention(qk, vx, bias, g_att, rows_per_step=16, rows_per_iter=8):
    bsz, s, _ = qk.shape
    rows = s // GRID_W
    tq = rows_per_step * GRID_W
    kern = functools.partial(_attn_kernel, rows=rows, rows_per_step=rows_per_step,
                             rows_per_iter=rows_per_iter)
    return pl.pallas_call(
        kern,
        grid=(bsz, rows // rows_per_step),
        in_specs=[
            pl.BlockSpec((1, tq, D_ATT), lambda b, i: (b, i, 0)),
            pl.BlockSpec((1, s, D_ATT), lambda b, i: (b, 0, 1)),
            pl.BlockSpec((1, s, 2 * D_ATT), lambda b, i: (b, 0, 0)),
            pl.BlockSpec(bias.shape, lambda b, i: (0, 0, 0)),
            pl.BlockSpec((1, D_ATT), lambda b, i: (0, 0)),
        ],
        out_specs=pl.BlockSpec((1, tq, D_ATT), lambda b, i: (b, i, 0)),
        out_shape=jax.ShapeDtypeStruct((bsz, s, D_ATT), BF16),
        compiler_params=_params(("parallel", "arbitrary")),
        name="nbr_attention",
    )(qk, qk, vx, bias, g_att.reshape(1, D_ATT))


def _lru_au(xe_ref, a_ref, u_ref, cw_ref, cb_ref, wg_ref, bg_ref, lam_ref, *, bsz, tc, sub_t):
    c = a_ref.shape[1]
    z = -lam_ref[...]
    neg_c_softplus = -LRU_C * (jnp.maximum(z, 0.0) + jnp.log1p(jnp.exp(-jnp.abs(z))))
    sub = sub_t * bsz

    def body(sb, carry):
        r0 = pl.multiple_of(sb * sub, sub)
        xc = cb_ref[...] + cw_ref[0:1, :] * xe_ref[pl.ds(r0, sub), :]
        for j in range(1, cw_ref.shape[0]):
            xc = xc + cw_ref[j:j + 1, :] * xe_ref[pl.ds(r0 + j * bsz, sub), :]
        gates = jnp.dot(xc.astype(BF16), wg_ref[...], preferred_element_type=F32) + bg_ref[...]
        r_gate = _sigmoid(gates[:, :c])
        i_gate = _sigmoid(gates[:, c:])
        log_a = r_gate * neg_c_softplus
        a = jnp.exp(log_a)
        a_ref[pl.ds(r0, sub), :] = a
        one_minus_a2 = -jnp.tanh(log_a) * (a * a + 1.0)
        root = jnp.where(one_minus_a2 > 0.0, one_minus_a2 * lax.rsqrt(one_minus_a2), 0.0)
        u_ref[pl.ds(r0, sub), :] = root * (i_gate * xc)
        return carry

    lax.fori_loop(0, tc // sub_t, body, 0)


def _fill_halo(xe_ref, xp_ref, xc_ref, xn_ref, has_prev, has_next, *, bsz, tc, pad):
    left, right = pad
    hb = xp_ref.shape[0]
    prev = xp_ref[pl.ds(hb - left * bsz, left * bsz), :]
    xe_ref[pl.ds(0, left * bsz), :] = jnp.where(has_prev, prev, 0.0)
    xe_ref[pl.ds(left * bsz, tc * bsz), :] = xc_ref[...]
    nxt = xn_ref[pl.ds(0, right * bsz), :]
    xe_ref[pl.ds((left + tc) * bsz, right * bsz), :] = jnp.where(has_next, nxt, 0.0)


def _lru_fwd_kernel(xp_ref, xc_ref, xn_ref, cw_ref, cb_ref, wg_ref, bg_ref, lam_ref,
                    hf_ref, xe_ref, a_ref, u_ref, h_ref, *, bsz, tc, sub_t, pad, unroll):
    i = pl.program_id(0)
    nc = pl.num_programs(0)

    @pl.when(i == 0)
    def _():
        h_ref[...] = jnp.zeros_like(h_ref)

    _fill_halo(xe_ref, xp_ref, xc_ref, xn_ref, i > 0, i < nc - 1, bsz=bsz, tc=tc, pad=pad)
    _lru_au(xe_ref, a_ref, u_ref, cw_ref, cb_ref, wg_ref, bg_ref, lam_ref,
            bsz=bsz, tc=tc, sub_t=sub_t)

    def step(t, h):
        rows = pl.ds(pl.multiple_of(t * bsz, bsz), bsz)
        h = a_ref[rows, :] * h + u_ref[rows, :]
        hf_ref[rows, :] = h
        return h

    h_ref[...] = lax.fori_loop(0, tc, step, h_ref[...], unroll=unroll)


def _lru_bwd_kernel(xp_ref, xc_ref, xn_ref, gate_ref, hf_ref, cw_ref, cb_ref, wg_ref, bg_ref,
                    lam_ref, grec_ref, perm_ref, o_ref, xe_ref, a_ref, u_ref, h_ref,
                    *, bsz, tc, sub_t, pad, unroll):
    i = pl.program_id(0)
    nc = pl.num_programs(0)

    @pl.when(i == 0)
    def _():
        h_ref[...] = jnp.zeros_like(h_ref)

    _fill_halo(xe_ref, xp_ref, xc_ref, xn_ref, i < nc - 1, i > 0, bsz=bsz, tc=tc, pad=pad)
    _lru_au(xe_ref, a_ref, u_ref, cw_ref, cb_ref, wg_ref, bg_ref, lam_ref,
            bsz=bsz, tc=tc, sub_t=sub_t)

    def step(k, h):
        t = tc - 1 - k
        rows = pl.ds(pl.multiple_of(t * bsz, bsz), bsz)
        h = a_ref[rows, :] * h + u_ref[rows, :]
        u_ref[rows, :] = h
        return h

    h_ref[...] = lax.fori_loop(0, tc, step, h_ref[...], unroll=unroll)

    sub = sub_t * bsz

    def finish(sb, carry):
        rows = pl.ds(pl.multiple_of(sb * sub, sub), sub)
        rec = (hf_ref[rows, :] + u_ref[rows, :]) * _gelu(gate_ref[rows, :])
        rec_n = _rms(rec, grec_ref[...]).astype(o_ref.dtype)
        rec_b = jnp.dot(perm_ref[...], rec_n, preferred_element_type=F32).astype(o_ref.dtype)
        t0 = pl.multiple_of(sb * sub_t, sub_t)
        for b in range(bsz):
            o_ref[b, pl.ds(t0, sub_t), :] = rec_b[b * sub_t:(b + 1) * sub_t]
        return carry

    lax.fori_loop(0, tc // sub_t, finish, 0)


def _lru(rg_t, cw, cb, wg, bg, lam, g_rec, *, bsz, tc=256, sub_t=32, unroll=8):
    rows, c2 = rg_t.shape
    c = c2 // 2
    s = rows // bsz
    nc = s // tc
    k = cw.shape[0]
    pad = (k // 2, k - 1 - k // 2)
    halo_t = 8
    hb = halo_t * bsz
    n_halo = s // halo_t
    cpb = tc // halo_t
    common = dict(bsz=bsz, tc=tc, sub_t=sub_t, pad=pad, unroll=unroll)
    scratch = [
        pltpu.VMEM(((tc + k - 1) * bsz, c), F32),
        pltpu.VMEM((tc * bsz, c), F32),
        pltpu.VMEM((tc * bsz, c), F32),
        pltpu.VMEM((bsz, c), F32),
    ]

    def x_specs(chunk):
        return [
            pl.BlockSpec((hb, c), lambda i: (jnp.maximum(chunk(i) * cpb - 1, 0), 0)),
            pl.BlockSpec((tc * bsz, c), lambda i: (chunk(i), 0)),
            pl.BlockSpec((hb, c), lambda i: (jnp.minimum((chunk(i) + 1) * cpb, n_halo - 1), 0)),
        ]

    def p_specs():
        return [
            pl.BlockSpec((k, c), lambda i: (0, 0)),
            pl.BlockSpec((1, c), lambda i: (0, 0)),
            pl.BlockSpec((c, 2 * c), lambda i: (0, 0)),
            pl.BlockSpec((1, 2 * c), lambda i: (0, 0)),
            pl.BlockSpec((1, c), lambda i: (0, 0)),
        ]

    fwd = lambda i: i
    hf = pl.pallas_call(
        functools.partial(_lru_fwd_kernel, **common),
        grid=(nc,),
        in_specs=x_specs(fwd) + p_specs(),
        out_specs=pl.BlockSpec((tc * bsz, c), lambda i: (i, 0)),
        out_shape=jax.ShapeDtypeStruct((rows, c), F32),
        scratch_shapes=scratch,
        compiler_params=_params(("arbitrary",)),
        name="lru_fwd",
    )(rg_t, rg_t, rg_t, cw, cb.reshape(1, c), wg[0], bg[0].reshape(1, 2 * c), lam[0].reshape(1, c))

    bwd = lambda i: nc - 1 - i
    perm = _row_permutation(bsz, sub_t)
    return pl.pallas_call(
        functools.partial(_lru_bwd_kernel, **common),
        grid=(nc,),
        in_specs=x_specs(bwd) + [
            pl.BlockSpec((tc * bsz, c), lambda i: (bwd(i), 1)),
            pl.BlockSpec((tc * bsz, c), lambda i: (bwd(i), 0)),
        ] + p_specs() + [pl.BlockSpec((1, c), lambda i: (0, 0)),
                         pl.BlockSpec(perm.shape, lambda i: (0, 0))],
        out_specs=pl.BlockSpec((bsz, tc, c), lambda i: (0, bwd(i), 0)),
        out_shape=jax.ShapeDtypeStruct((bsz, s, c), BF16),
        scratch_shapes=scratch,
        compiler_params=_params(("arbitrary",)),
        name="lru_bwd",
    )(rg_t, rg_t, rg_t, rg_t, hf, cw, cb.reshape(1, c), wg[1], bg[1].reshape(1, 2 * c),
      lam[1].reshape(1, c), g_rec.reshape(1, c), perm)


def _gate_weights(lru_w):
    nd, ng, nb, bs, _ = lru_w.shape
    eye = jnp.eye(nb, dtype=lru_w.dtype)
    dense = jnp.einsum("xgncd,nm->xgncmd", lru_w, eye).reshape(nd, ng, nb * bs, nb * bs)
    return jnp.concatenate([dense[:, 0], dense[:, 1]], axis=-1)


def _out_proj_kernel(x_ref, att_ref, rec_ref, wo_ref, g_ref, x1_ref, hn_ref):
    mixed = jnp.concatenate([att_ref[...], rec_ref[...]], axis=-1)
    x1 = x_ref[...] + jnp.dot(mixed, wo_ref[...], preferred_element_type=F32)
    x1_ref[...] = x1
    hn_ref[...] = _rms(x1, g_ref[...]).astype(hn_ref.dtype)


def _out_proj(x, att, rec, wo, g, tm=1024):
    n, d = x.shape
    c = att.shape[1]
    return pl.pallas_call(
        _out_proj_kernel,
        grid=(n // tm,),
        in_specs=[
            pl.BlockSpec((tm, d), lambda i: (i, 0)),
            pl.BlockSpec((tm, c), lambda i: (i, 0)),
            pl.BlockSpec((tm, c), lambda i: (i, 0)),
            pl.BlockSpec(wo.shape, lambda i: (0, 0)),
            pl.BlockSpec((1, d), lambda i: (0, 0)),
        ],
        out_specs=[
            pl.BlockSpec((tm, d), lambda i: (i, 0)),
            pl.BlockSpec((tm, d), lambda i: (i, 0)),
        ],
        out_shape=[
            jax.ShapeDtypeStruct((n, d), F32),
            jax.ShapeDtypeStruct((n, d), BF16),
        ],
        compiler_params=_params(("parallel",)),
        name="out_proj",
    )(x, att, rec, wo, g.reshape(1, d))


HALO = 16


def _ffn_kernel(hc_ref, hp_ref, hx_ref, x1_ref, wu_ref, cw_ref, cb_ref, wd_ref, g_ref, *out_refs,
                n_chunks, last):
    j = pl.program_id(1)
    nj = pl.num_programs(1)
    tm = hc_ref.shape[1]
    f = wd_ref.shape[0]
    wc = f // n_chunks
    hn = hc_ref[0]
    prev = jnp.where(j > 0, hp_ref[0], jnp.zeros_like(hp_ref[0]))
    nxt = jnp.where(j < nj - 1, hx_ref[0], jnp.zeros_like(hx_ref[0]))
    hn_ext = jnp.concatenate([prev, hn, nxt], axis=0)
    ext = tm + 2 * HALO
    gelu_c = math.sqrt(2.0 / math.pi)
    acc = x1_ref[0]

    def up_proj(ch):
        return (jnp.dot(hn_ext, wu_ref[:, pl.ds(ch * wc, wc)], preferred_element_type=F32),
                jnp.dot(hn, wu_ref[:, pl.ds(f + ch * wc, wc)], preferred_element_type=F32))

    pending = up_proj(0)
    for ch in range(n_chunks):
        cols = pl.ds(ch * wc, wc)
        ua_ext, ul = pending
        if ch + 1 < n_chunks:
            pending = up_proj(ch + 1)
        before = pltpu.roll(ua_ext, 1, axis=0)[HALO:HALO + tm]
        after = pltpu.roll(ua_ext, ext - 1, axis=0)[HALO:HALO + tm]
        conv = (cb_ref[:, cols] + cw_ref[0:1, cols] * before + cw_ref[1:2, cols] * ua_ext[HALO:HALO + tm]
                + cw_ref[2:3, cols] * after)
        inner = conv * (conv * conv * (gelu_c * 0.044715) + gelu_c)
        z = (0.5 * conv) * ul
        act = (z + z * jnp.tanh(inner)).astype(BF16)
        acc = acc + jnp.dot(act, wd_ref[cols, :], preferred_element_type=F32)
    if last:
        out_refs[0][0] = _rms(acc, g_ref[...])
    else:
        out_refs[0][0] = acc
        out_refs[1][0] = _rms(acc, g_ref[...]).astype(out_refs[1].dtype)


def _ffn(hn, x1, wu, cw, cb, wd, g_next, *, last, tm=256):
    bsz, s, d = x1.shape
    f = wd.shape[0]
    tpb = tm // HALO
    n_hb = s // HALO
    tile = pl.BlockSpec((1, tm, d), lambda b, j: (b, j, 0))
    if last:
        out_specs = [tile]
        out_shape = [jax.ShapeDtypeStruct((bsz, s, d), F32)]
    else:
        out_specs = [tile, tile]
        out_shape = [jax.ShapeDtypeStruct((bsz, s, d), F32), jax.ShapeDtypeStruct((bsz, s, d), BF16)]
    return pl.pallas_call(
        functools.partial(_ffn_kernel, n_chunks=6, last=last),
        grid=(bsz, s // tm),
        in_specs=[
            tile,
            pl.BlockSpec((1, HALO, d), lambda b, j: (b, jnp.maximum(j * tpb - 1, 0), 0)),
            pl.BlockSpec((1, HALO, d), lambda b, j: (b, jnp.minimum((j + 1) * tpb, n_hb - 1), 0)),
            tile,
            pl.BlockSpec(wu.shape, lambda b, j: (0, 0)),
            pl.BlockSpec(cw.shape, lambda b, j: (0, 0)),
            pl.BlockSpec((1, f), lambda b, j: (0, 0)),
            pl.BlockSpec(wd.shape, lambda b, j: (0, 0)),
            pl.BlockSpec((1, d), lambda b, j: (0, 0)),
        ],
        out_specs=out_specs,
        out_shape=out_shape,
        compiler_params=_params(("parallel", "arbitrary")),
        name="ffn",
    )(hn, hn, hn, x1, wu, cw, cb.reshape(1, f), wd, g_next.reshape(1, d))


def kernel(x, g_mix, w_in, rel_bias, conv_lru_w, conv_lru_b, lru_w, lru_b, lru_lam, g_att, g_rec,
           w_out, g_ffn, w_up, conv_ffn_w, conv_ffn_b, w_down, g_final):
    bsz, s, d = x.shape
    depth = w_in.shape[0]
    n = bsz * s
    c_lru = conv_lru_w.shape[-1]
    w_in_b = w_in.astype(BF16)
    w_out_b = w_out.astype(BF16)
    w_up_b = w_up.astype(BF16)
    w_down_b = w_down.astype(BF16)
    hn = _norm_cast(x.reshape(n, d), g_mix[0]).reshape(bsz, s, d)
    for l in range(depth):
        last = l == depth - 1
        qk, vx, rg_t = _in_proj(hn, w_in_b[l])
        att = _attention(qk, vx, _bias_pieces(rel_bias[l]), g_att[l])
        wg = _gate_weights(lru_w[l]).astype(BF16)
        bg = jnp.concatenate([lru_b[l, :, 0], lru_b[l, :, 1]], axis=-1)
        rec = _lru(rg_t, conv_lru_w[l], conv_lru_b[l], wg, bg, lru_lam[l], g_rec[l], bsz=bsz)
        x1, hf = _out_proj(x.reshape(n, d), att.reshape(n, D_ATT), rec.reshape(n, c_lru), w_out_b[l],
                           g_ffn[l])
        outs = _ffn(hf.reshape(bsz, s, d), x1.reshape(bsz, s, d), w_up_b[l], conv_ffn_w[l],
                    conv_ffn_b[l], w_down_b[l], g_final if last else g_mix[l + 1], last=last)
        if last:
            return outs[0]
        x, hn = outs
```

```python
import functools
import math

import jax
import jax.numpy as jnp
from jax import lax
from jax.experimental import pallas as pl
from jax.experimental.pallas import tpu as pltpu

F32 = jnp.float32
BF16 = jnp.bfloat16

GRID_W = 64
N_HEADS = 8
HEAD_DIM = 64
D_ATT = N_HEADS * HEAD_DIM
WIN_H = 8
WIN_W = 16
LRU_C = 8.0
EPS = 1e-6
MASKED = -1e30
VMEM_LIMIT = 56 * 1024 * 1024
HALO = 16


def _rms(x, g):
    ms = jnp.mean(x * x, axis=-1, keepdims=True)
    return x * lax.rsqrt(ms + EPS) * g


def _gelu(x):
    c = math.sqrt(2.0 / math.pi)
    return 0.5 * x * (1.0 + jnp.tanh(c * (x + 0.044715 * (x * x * x))))


def _sigmoid(x):
    return 0.5 + 0.5 * jnp.tanh(0.5 * x)


def _params(sem):
    return pltpu.CompilerParams(dimension_semantics=sem, vmem_limit_bytes=VMEM_LIMIT)


def _row_permutation(n_outer, n_inner):
    r = jnp.arange(n_outer * n_inner)
    src = (r % n_inner) * n_outer + r // n_inner
    return jax.nn.one_hot(src, n_outer * n_inner, dtype=BF16)


def _in_proj_body(hn3, p_ref, w_ref, qk_ref, vx_ref, rg_ref):
    nb, tt, d = hn3.shape
    hn = hn3.reshape(nb * tt, d)
    pair = 2 * HEAD_DIM
    q = jnp.dot(hn, w_ref[:, :D_ATT], preferred_element_type=F32)
    qk_ref[:, :, :D_ATT] = (q * (HEAD_DIM ** -0.5)).astype(BF16).reshape(nb, tt, D_ATT)
    k = jnp.dot(hn, w_ref[:, D_ATT:2 * D_ATT], preferred_element_type=F32)
    qk_ref[:, :, D_ATT:] = k.astype(BF16).reshape(nb, tt, D_ATT)
    v = jnp.dot(hn, w_ref[:, 2 * D_ATT:3 * D_ATT], preferred_element_type=F32).astype(BF16)
    ones = jnp.ones((nb, tt, pair), BF16)
    for p in range(D_ATT // pair):
        vx_ref[:, :, 2 * p * pair:(2 * p + 1) * pair] = v[:, p * pair:(p + 1) * pair].reshape(nb, tt, pair)
        vx_ref[:, :, (2 * p + 1) * pair:(2 * p + 2) * pair] = ones
    parts = []
    for t0 in range(0, tt, HALO):
        rows = hn3[:, t0:t0 + HALO, :].reshape(nb * HALO, d)
        parts.append(jnp.dot(p_ref[...], rows, preferred_element_type=F32).astype(BF16))
    hn_t = jnp.concatenate(parts, axis=0)
    rg_ref[...] = jnp.dot(hn_t, w_ref[:, 3 * D_ATT:], preferred_element_type=F32)


def _in_proj_first_kernel(x_ref, g_ref, p_ref, w_ref, qk_ref, vx_ref, rg_ref):
    _in_proj_body(_rms(x_ref[...], g_ref[...]).astype(BF16), p_ref, w_ref, qk_ref, vx_ref, rg_ref)


def _in_proj_kernel(h_ref, p_ref, w_ref, qk_ref, vx_ref, rg_ref):
    _in_proj_body(h_ref[...], p_ref, w_ref, qk_ref, vx_ref, rg_ref)


def _in_proj(h, w_all, layer, g_first=None, tt=64):
    bsz, s, d = h.shape
    d_in = w_all.shape[2]
    d_rg = d_in - 3 * D_ATT
    perm = _row_permutation(HALO, bsz)
    bm_tile = pl.BlockSpec((bsz, tt, 2 * D_ATT), lambda i: (0, i, 0))
    h_spec = pl.BlockSpec((bsz, tt, d), lambda i: (0, i, 0))
    rest = [pl.BlockSpec(perm.shape, lambda i: (0, 0)),
            pl.BlockSpec((None, d, d_in), lambda i: (layer, 0, 0))]
    if g_first is None:
        body, in_specs, args = _in_proj_kernel, [h_spec] + rest, (h, perm, w_all)
    else:
        body = _in_proj_first_kernel
        in_specs = [h_spec, pl.BlockSpec((1, d), lambda i: (0, 0))] + rest
        args = (h, g_first.reshape(1, d), perm, w_all)
    return pl.pallas_call(
        body,
        grid=(s // tt,),
        in_specs=in_specs,
        out_specs=[bm_tile, bm_tile, pl.BlockSpec((tt * bsz, d_rg), lambda i: (i, 0))],
        out_shape=[
            jax.ShapeDtypeStruct((bsz, s, 2 * D_ATT), BF16),
            jax.ShapeDtypeStruct((bsz, s, 2 * D_ATT), BF16),
            jax.ShapeDtypeStruct((s * bsz, d_rg), F32),
        ],
        compiler_params=_params(("parallel",)),
        name="in_proj",
    )(*args)


def _bias_rows(rel_bias):
    h = rel_bias.shape[0]
    z = jnp.zeros((h, 2 * WIN_H - 2, 33), rel_bias.dtype)
    return jnp.concatenate(
        [rel_bias[:, :-1, WIN_W - 1:], z, rel_bias[:, 1:, :], z, rel_bias[:, :-1, :WIN_W - 1]], axis=-1)


def _bias_kernel(rows_ref, o_ref):
    n_dr = o_ref.shape[0]
    shape = (GRID_W, 2 * GRID_W)
    c = lax.broadcasted_iota(jnp.int32, shape, 0)
    kc = lax.broadcasted_iota(jnp.int32, shape, 1) & (GRID_W - 1)
    col_start = jnp.clip(c - WIN_W // 2, 0, GRID_W - WIN_W)
    valid = (kc >= col_start) & (kc < col_start + WIN_W)
    for h in range(N_HEADS):
        for dr in range(n_dr):
            row = jnp.broadcast_to(rows_ref[h, dr:dr + 1, :], shape)
            toeplitz = pltpu.roll(row, 0, 1, stride=1, stride_axis=0)
            o_ref[dr, h * GRID_W:(h + 1) * GRID_W, :] = jnp.where(valid, toeplitz, MASKED)


def _bias_pieces(rel_bias):
    rows = _bias_rows(rel_bias.astype(F32))
    n_dr = rows.shape[1]
    return pl.pallas_call(
        _bias_kernel,
        out_shape=jax.ShapeDtypeStruct((n_dr, N_HEADS * GRID_W, 2 * GRID_W), F32),
        name="bias_pieces",
    )(rows)


def _attn_kernel(q_ref, k_ref, v_ref, bias_ref, g_ref, o_ref, *, rows, rows_per_step, rows_per_iter):
    i = pl.program_id(1)
    band = WIN_H * GRID_W
    pair = 2 * HEAD_DIM
    group = 4 * HEAD_DIM
    n_groups = D_ATT // group
    lane = lax.broadcasted_iota(jnp.int32, (GRID_W, group), 1)
    head_lanes = [(lane >= h * HEAD_DIM) & (lane < (h + 1) * HEAD_DIM) for h in range(4)]
    low_half = lax.broadcasted_iota(jnp.int32, (GRID_W, pair), 1) < HEAD_DIM
    g = g_ref[...]

    def qk_stage(r, q0):
        row_start = jnp.clip(r - WIN_H // 2, 0, rows - WIN_H)
        off = r - row_start
        k0 = pl.multiple_of(row_start * GRID_W, GRID_W)
        scores = []
        for gi in range(n_groups):
            lanes = pl.ds(gi * group, group)
            q4 = q_ref[0, pl.ds(q0, GRID_W), lanes]
            qm = jnp.concatenate([jnp.where(m, q4, jnp.zeros_like(q4)) for m in head_lanes], axis=0)
            k4 = k_ref[0, pl.ds(k0, band), lanes]
            sc = lax.dot_general(qm, k4, (((1,), (1,)), ((), ())), preferred_element_type=F32)
            bias = jnp.concatenate(
                [bias_ref[(WIN_H - 1) - off + 2 * ii, pl.ds(gi * group, group), :]
                 for ii in range(WIN_H // 2)], axis=1)
            scores.append(sc + bias)
        return k0, scores

    def softmax_stage(scores):
        return [jnp.exp(sc - jnp.max(sc, axis=-1, keepdims=True)).astype(BF16) for sc in scores]

    def pv_stage(k0, q0, probs):
        outs = []
        for gi in range(n_groups):
            for pp in range(2):
                e2 = probs[gi][pp * pair:(pp + 1) * pair, :]
                vx = v_ref[0, pl.ds(k0, band), pl.ds((2 * gi + pp) * 2 * pair, 2 * pair)]
                pv = jnp.dot(e2, vx, preferred_element_type=F32)
                o = pv[:, :pair] / pv[:, pair:]
                outs.append(jnp.where(low_half, o[:GRID_W], o[GRID_W:]))
        att = jnp.concatenate(outs, axis=-1)
        o_ref[0, pl.ds(q0, GRID_W), :] = _rms(att, g).astype(o_ref.dtype)

    def iter_body(j, carry):
        base = j * rows_per_iter
        q0s = [pl.multiple_of((base + t) * GRID_W, GRID_W) for t in range(rows_per_iter)]
        k0s, scs, probs = {}, {}, {}
        for t in range(rows_per_iter + 2):
            if t < rows_per_iter:
                k0s[t], scs[t] = qk_stage(i * rows_per_step + base + t, q0s[t])
            if 1 <= t <= rows_per_iter:
                probs[t - 1] = softmax_stage(scs.pop(t - 1))
            if t >= 2:
                pv_stage(k0s[t - 2], q0s[t - 2], probs.pop(t - 2))
        return carry

    lax.fori_loop(0, rows_per_step // rows_per_iter, iter_body, 0)


def _attention(qk, vx, bias, g_att, rows_per_step=16, rows_per_iter=8):
    bsz, s, _ = qk.shape
    rows = s // GRID_W
    tq = rows_per_step * GRID_W
    kern = functools.partial(_attn_kernel, rows=rows, rows_per_step=rows_per_step,
                             rows_per_iter=rows_per_iter)
    return pl.pallas_call(
        kern,
        grid=(bsz, rows // rows_per_step),
        in_specs=[
            pl.BlockSpec((1, tq, D_ATT), lambda b, i: (b, i, 0)),
            pl.BlockSpec((1, s, D_ATT), lambda b, i: (b, 0, 1)),
            pl.BlockSpec((1, s, 2 * D_ATT), lambda b, i: (b, 0, 0)),
            pl.BlockSpec(bias.shape, lambda b, i: (0, 0, 0)),
            pl.BlockSpec((1, D_ATT), lambda b, i: (0, 0)),
        ],
        out_specs=pl.BlockSpec((1, tq, D_ATT), lambda b, i: (b, i, 0)),
        out_shape=jax.ShapeDtypeStruct((bsz, s, D_ATT), BF16),
        compiler_params=_params(("parallel", "arbitrary")),
        name="nbr_attention",
    )(qk, qk, vx, bias, g_att.reshape(1, D_ATT))


def _lru_au(xe_ref, a_ref, u_ref, cw_ref, cb_ref, wg_ref, bg_ref, lam_ref, *, bsz, tc, sub_t):
    c = a_ref.shape[1]
    z = -lam_ref[...]
    neg_c_softplus = -LRU_C * (jnp.maximum(z, 0.0) + jnp.log1p(jnp.exp(-jnp.abs(z))))
    sub = sub_t * bsz

    def body(sb, carry):
        r0 = pl.multiple_of(sb * sub, sub)
        xc = cb_ref[...] + cw_ref[0:1, :] * xe_ref[pl.ds(r0, sub), :]
        for j in range(1, cw_ref.shape[0]):
            xc = xc + cw_ref[j:j + 1, :] * xe_ref[pl.ds(r0 + j * bsz, sub), :]
        gates = jnp.dot(xc.astype(BF16), wg_ref[...], preferred_element_type=F32) + bg_ref[...]
        r_gate = _sigmoid(gates[:, :c])
        i_gate = _sigmoid(gates[:, c:])
        log_a = r_gate * neg_c_softplus
        a = jnp.exp(log_a)
        a_ref[pl.ds(r0, sub), :] = a
        one_minus_a2 = -jnp.tanh(log_a) * (a * a + 1.0)
        root = jnp.where(one_minus_a2 > 0.0, one_minus_a2 * lax.rsqrt(one_minus_a2), 0.0)
        u_ref[pl.ds(r0, sub), :] = root * (i_gate * xc)
        return carry

    lax.fori_loop(0, tc // sub_t, body, 0)


def _fill_halo(xe_ref, xp_ref, xc_ref, xn_ref, has_prev, has_next, *, bsz, tc, pad):
    left, right = pad
    hb = xp_ref.shape[0]
    prev = xp_ref[pl.ds(hb - left * bsz, left * bsz), :]
    xe_ref[pl.ds(0, left * bsz), :] = jnp.where(has_prev, prev, 0.0)
    xe_ref[pl.ds(left * bsz, tc * bsz), :] = xc_ref[...]
    nxt = xn_ref[pl.ds(0, right * bsz), :]
    xe_ref[pl.ds((left + tc) * bsz, right * bsz), :] = jnp.where(has_next, nxt, 0.0)


def _lru_fwd_kernel(xp_ref, xc_ref, xn_ref, cw_ref, cb_ref, wg_ref, bg_ref, lam_ref,
                    hf_ref, xe_ref, a_ref, u_ref, h_ref, *, bsz, tc, sub_t, pad, unroll):
    i = pl.program_id(0)
    nc = pl.num_programs(0)

    @pl.when(i == 0)
    def _():
        h_ref[...] = jnp.zeros_like(h_ref)

    _fill_halo(xe_ref, xp_ref, xc_ref, xn_ref, i > 0, i < nc - 1, bsz=bsz, tc=tc, pad=pad)
    _lru_au(xe_ref, a_ref, u_ref, cw_ref, cb_ref, wg_ref, bg_ref, lam_ref,
            bsz=bsz, tc=tc, sub_t=sub_t)

    def step(t, h):
        rows = pl.ds(pl.multiple_of(t * bsz, bsz), bsz)
        h = a_ref[rows, :] * h + u_ref[rows, :]
        hf_ref[rows, :] = h
        return h

    h_ref[...] = lax.fori_loop(0, tc, step, h_ref[...], unroll=unroll)


def _lru_bwd_kernel(xp_ref, xc_ref, xn_ref, gate_ref, hf_ref, cw_ref, cb_ref, wg_ref, bg_ref,
                    lam_ref, grec_ref, perm_ref, o_ref, xe_ref, a_ref, u_ref, h_ref,
                    *, bsz, tc, sub_t, pad, unroll):
    i = pl.program_id(0)
    nc = pl.num_programs(0)

    @pl.when(i == 0)
    def _():
        h_ref[...] = jnp.zeros_like(h_ref)

    _fill_halo(xe_ref, xp_ref, xc_ref, xn_ref, i < nc - 1, i > 0, bsz=bsz, tc=tc, pad=pad)
    _lru_au(xe_ref, a_ref, u_ref, cw_ref, cb_ref, wg_ref, bg_ref, lam_ref,
            bsz=bsz, tc=tc, sub_t=sub_t)

    def step(k, h):
        t = tc - 1 - k
        rows = pl.ds(pl.multiple_of(t * bsz, bsz), bsz)
        h = a_ref[rows, :] * h + u_ref[rows, :]
        u_ref[rows, :] = h
        return h

    h_ref[...] = lax.fori_loop(0, tc, step, h_ref[...], unroll=unroll)

    sub = sub_t * bsz

    def finish(sb, carry):
        rows = pl.ds(pl.multiple_of(sb * sub, sub), sub)
        rec = (hf_ref[rows, :] + u_ref[rows, :]) * _gelu(gate_ref[rows, :])
        rec_n = _rms(rec, grec_ref[...]).astype(o_ref.dtype)
        rec_b = jnp.dot(perm_ref[...], rec_n, preferred_element_type=F32).astype(o_ref.dtype)
        t0 = pl.multiple_of(sb * sub_t, sub_t)
        for b in range(bsz):
            o_ref[b, pl.ds(t0, sub_t), :] = rec_b[b * sub_t:(b + 1) * sub_t]
        return carry

    lax.fori_loop(0, tc // sub_t, finish, 0)


def _lru(rg_t, cw, cb, wg, bg, lam, g_rec, *, bsz, tc=256, sub_t=32, unroll=8):
    rows, c2 = rg_t.shape
    c = c2 // 2
    s = rows // bsz
    nc = s // tc
    k = cw.shape[0]
    pad = (k // 2, k - 1 - k // 2)
    halo_t = 8
    hb = halo_t * bsz
    n_halo = s // halo_t
    cpb = tc // halo_t
    common = dict(bsz=bsz, tc=tc, sub_t=sub_t, pad=pad, unroll=unroll)
    scratch = [
        pltpu.VMEM(((tc + k - 1) * bsz, c), F32),
        pltpu.VMEM((tc * bsz, c), F32),
        pltpu.VMEM((tc * bsz, c), F32),
        pltpu.VMEM((bsz, c), F32),
    ]

    def x_specs(chunk):
        return [
            pl.BlockSpec((hb, c), lambda i: (jnp.maximum(chunk(i) * cpb - 1, 0), 0)),
            pl.BlockSpec((tc * bsz, c), lambda i: (chunk(i), 0)),
            pl.BlockSpec((hb, c), lambda i: (jnp.minimum((chunk(i) + 1) * cpb, n_halo - 1), 0)),
        ]

    def p_specs():
        return [
            pl.BlockSpec((k, c), lambda i: (0, 0)),
            pl.BlockSpec((1, c), lambda i: (0, 0)),
            pl.BlockSpec((c, 2 * c), lambda i: (0, 0)),
            pl.BlockSpec((1, 2 * c), lambda i: (0, 0)),
            pl.BlockSpec((1, c), lambda i: (0, 0)),
        ]

    fwd = lambda i: i
    hf = pl.pallas_call(
        functools.partial(_lru_fwd_kernel, **common),
        grid=(nc,),
        in_specs=x_specs(fwd) + p_specs(),
        out_specs=pl.BlockSpec((tc * bsz, c), lambda i: (i, 0)),
        out_shape=jax.ShapeDtypeStruct((rows, c), F32),
        scratch_shapes=scratch,
        compiler_params=_params(("arbitrary",)),
        name="lru_fwd",
    )(rg_t, rg_t, rg_t, cw, cb.reshape(1, c), wg[0], bg[0].reshape(1, 2 * c), lam[0].reshape(1, c))

    bwd = lambda i: nc - 1 - i
    perm = _row_permutation(bsz, sub_t)
    return pl.pallas_call(
        functools.partial(_lru_bwd_kernel, **common),
        grid=(nc,),
        in_specs=x_specs(bwd) + [
            pl.BlockSpec((tc * bsz, c), lambda i: (bwd(i), 1)),
            pl.BlockSpec((tc * bsz, c), lambda i: (bwd(i), 0)),
        ] + p_specs() + [pl.BlockSpec((1, c), lambda i: (0, 0)),
                         pl.BlockSpec(perm.shape, lambda i: (0, 0))],
        out_specs=pl.BlockSpec((bsz, tc, c), lambda i: (0, bwd(i), 0)),
        out_shape=jax.ShapeDtypeStruct((bsz, s, c), BF16),
        scratch_shapes=scratch,
        compiler_params=_params(("arbitrary",)),
        name="lru_bwd",
    )(rg_t, rg_t, rg_t, rg_t, hf, cw, cb.reshape(1, c), wg[1], bg[1].reshape(1, 2 * c),
      lam[1].reshape(1, c), g_rec.reshape(1, c), perm)


def _gate_weights(lru_w):
    nd, ng, nb, bs, _ = lru_w.shape
    eye = jnp.eye(nb, dtype=lru_w.dtype)
    dense = jnp.einsum("xgncd,nm->xgncmd", lru_w, eye).reshape(nd, ng, nb * bs, nb * bs)
    return jnp.concatenate([dense[:, 0], dense[:, 1]], axis=-1)


def _out_proj_kernel(x_ref, att_ref, rec_ref, wo_ref, g_ref, x1_ref, hn_ref):
    mixed = jnp.concatenate([att_ref[...], rec_ref[...]], axis=-1)
    x1 = x_ref[...] + jnp.dot(mixed, wo_ref[...], preferred_element_type=F32)
    x1_ref[...] = x1
    hn_ref[...] = _rms(x1, g_ref[...]).astype(hn_ref.dtype)


def _out_proj(x, att, rec, wo_all, layer, g, tm=1024):
    n, d = x.shape
    c = att.shape[1]
    return pl.pallas_call(
        _out_proj_kernel,
        grid=(n // tm,),
        in_specs=[
            pl.BlockSpec((tm, d), lambda i: (i, 0)),
            pl.BlockSpec((tm, c), lambda i: (i, 0)),
            pl.BlockSpec((tm, c), lambda i: (i, 0)),
            pl.BlockSpec((None,) + wo_all.shape[1:], lambda i: (layer, 0, 0)),
            pl.BlockSpec((1, d), lambda i: (0, 0)),
        ],
        out_specs=[
            pl.BlockSpec((tm, d), lambda i: (i, 0)),
            pl.BlockSpec((tm, d), lambda i: (i, 0)),
        ],
        out_shape=[
            jax.ShapeDtypeStruct((n, d), F32),
            jax.ShapeDtypeStruct((n, d), BF16),
        ],
        compiler_params=_params(("parallel",)),
        name="out_proj",
    )(x, att, rec, wo_all, g.reshape(1, d))


def _ffn_kernel(hc_ref, hp_ref, hx_ref, x1_ref, wu_ref, cw_ref, cb_ref, wd_ref, g_ref, *out_refs,
                n_chunks, sub_rows, last):
    j = pl.program_id(1)
    nj = pl.num_programs(1)
    tm = hc_ref.shape[1]
    f = wd_ref.shape[0]
    wc = f // n_chunks
    n_sub = tm // sub_rows
    prev = jnp.where(j > 0, hp_ref[0], jnp.zeros_like(hp_ref[0]))
    nxt = jnp.where(j < nj - 1, hx_ref[0], jnp.zeros_like(hx_ref[0]))
    hn_ext = jnp.concatenate([prev, hc_ref[0], nxt], axis=0)
    ext = sub_rows + 2 * HALO
    gelu_c = math.sqrt(2.0 / math.pi)

    def up_proj(stage):
        k, ch = divmod(stage, n_chunks)
        rows_ext = hn_ext[k * sub_rows:k * sub_rows + ext]
        rows = hn_ext[k * sub_rows + HALO:(k + 1) * sub_rows + HALO]
        return (jnp.dot(rows_ext, wu_ref[:, pl.ds(ch * wc, wc)], preferred_element_type=F32),
                jnp.dot(rows, wu_ref[:, pl.ds(f + ch * wc, wc)], preferred_element_type=F32))

    n_stages = n_sub * n_chunks
    pending = up_proj(0)
    acc = None
    for stage in range(n_stages):
        k, ch = divmod(stage, n_chunks)
        rows = pl.ds(k * sub_rows, sub_rows)
        cols = pl.ds(ch * wc, wc)
        ua_ext, ul = pending
        if stage + 1 < n_stages:
            pending = up_proj(stage + 1)
        if ch == 0:
            acc = x1_ref[0, rows, :]
        before = pltpu.roll(ua_ext, 1, axis=0)[HALO:HALO + sub_rows]
        after = pltpu.roll(ua_ext, ext - 1, axis=0)[HALO:HALO + sub_rows]
        conv = (cb_ref[:, cols] + cw_ref[0:1, cols] * before
                + cw_ref[1:2, cols] * ua_ext[HALO:HALO + sub_rows] + cw_ref[2:3, cols] * after)
        inner = conv * (conv * conv * (gelu_c * 0.044715) + gelu_c)
        z = (0.5 * conv) * ul
        act = (z + z * jnp.tanh(inner)).astype(BF16)
        acc = acc + jnp.dot(act, wd_ref[cols, :], preferred_element_type=F32)
        if ch == n_chunks - 1:
            if last:
                out_refs[0][0, rows, :] = _rms(acc, g_ref[...])
            else:
                out_refs[0][0, rows, :] = acc
                out_refs[1][0, rows, :] = _rms(acc, g_ref[...]).astype(out_refs[1].dtype)


def _ffn(hn, x1, wu_all, cw, cb, wd_all, layer, g_next, *, last, tm=512, sub_rows=256):
    bsz, s, d = x1.shape
    f = wd_all.shape[1]
    tpb = tm // HALO
    n_hb = s // HALO
    tile = pl.BlockSpec((1, tm, d), lambda b, j: (b, j, 0))
    if last:
        out_specs = [tile]
        out_shape = [jax.ShapeDtypeStruct((bsz, s, d), F32)]
    else:
        out_specs = [tile, tile]
        out_shape = [jax.ShapeDtypeStruct((bsz, s, d), F32), jax.ShapeDtypeStruct((bsz, s, d), BF16)]
    return pl.pallas_call(
        functools.partial(_ffn_kernel, n_chunks=6, sub_rows=sub_rows, last=last),
        grid=(bsz, s // tm),
        in_specs=[
            tile,
            pl.BlockSpec((1, HALO, d), lambda b, j: (b, jnp.maximum(j * tpb - 1, 0), 0)),
            pl.BlockSpec((1, HALO, d), lambda b, j: (b, jnp.minimum((j + 1) * tpb, n_hb - 1), 0)),
            tile,
            pl.BlockSpec((None,) + wu_all.shape[1:], lambda b, j: (layer, 0, 0)),
            pl.BlockSpec(cw.shape, lambda b, j: (0, 0)),
            pl.BlockSpec((1, f), lambda b, j: (0, 0)),
            pl.BlockSpec((None,) + wd_all.shape[1:], lambda b, j: (layer, 0, 0)),
            pl.BlockSpec((1, d), lambda b, j: (0, 0)),
        ],
        out_specs=out_specs,
        out_shape=out_shape,
        compiler_params=_params(("parallel", "arbitrary")),
        name="ffn",
    )(hn, hn, hn, x1, wu_all, cw, cb.reshape(1, f), wd_all, g_next.reshape(1, d))


def kernel(x, g_mix, w_in, rel_bias, conv_lru_w, conv_lru_b, lru_w, lru_b, lru_lam, g_att, g_rec,
           w_out, g_ffn, w_up, conv_ffn_w, conv_ffn_b, w_down, g_final):
    bsz, s, d = x.shape
    depth = w_in.shape[0]
    n = bsz * s
    c_lru = conv_lru_w.shape[-1]
    w_in_b = w_in.astype(BF16)
    w_out_b = w_out.astype(BF16)
    w_up_b = w_up.astype(BF16)
    w_down_b = w_down.astype(BF16)
    hn = x
    for l in range(depth):
        last = l == depth - 1
        qk, vx, rg_t = _in_proj(hn, w_in_b, l, g_first=g_mix[0] if l == 0 else None)
        att = _attention(qk, vx, _bias_pieces(rel_bias[l]), g_att[l])
        wg = _gate_weights(lru_w[l]).astype(BF16)
        bg = jnp.concatenate([lru_b[l, :, 0], lru_b[l, :, 1]], axis=-1)
        rec = _lru(rg_t, conv_lru_w[l], conv_lru_b[l], wg, bg, lru_lam[l], g_rec[l], bsz=bsz)
        x1, hf = _out_proj(x.reshape(n, d), att.reshape(n, D_ATT), rec.reshape(n, c_lru), w_out_b, l,
                           g_ffn[l])
        outs = _ffn(hf.reshape(bsz, s, d), x1.reshape(bsz, s, d), w_up_b, conv_ffn_w[l],
                    conv_ffn_b[l], w_down_b, l, g_final if last else g_mix[l + 1], last=last)
        if last:
            return outs[0]
        x, hn = outs
```

```python
import functools
import math

import jax
import jax.numpy as jnp
from jax import lax
from jax.experimental import pallas as pl
from jax.experimental.pallas import tpu as pltpu

F32 = jnp.float32
BF16 = jnp.bfloat16

GRID_W = 64
N_HEADS = 8
HEAD_DIM = 64
D_ATT = N_HEADS * HEAD_DIM
WIN_H = 8
WIN_W = 16
LRU_C = 8.0
EPS = 1e-6
MASKED = -1e30
VMEM_LIMIT = 56 * 1024 * 1024
HALO = 16


def _rms(x, g):
    ms = jnp.mean(x * x, axis=-1, keepdims=True)
    return x * lax.rsqrt(ms + EPS) * g


def _gelu(x):
    c = math.sqrt(2.0 / math.pi)
    return 0.5 * x * (1.0 + jnp.tanh(c * (x + 0.044715 * (x * x * x))))


def _sigmoid(x):
    return 0.5 + 0.5 * jnp.tanh(0.5 * x)


def _params(sem):
    return pltpu.CompilerParams(dimension_semantics=sem, vmem_limit_bytes=VMEM_LIMIT)


def _row_permutation(n_outer, n_inner):
    r = jnp.arange(n_outer * n_inner)
    src = (r % n_inner) * n_outer + r // n_inner
    return jax.nn.one_hot(src, n_outer * n_inner, dtype=BF16)


def _in_proj_body(hn3, p_ref, w_ref, qk_ref, vx_ref, rg_ref):
    nb, tt, d = hn3.shape
    hn = hn3.reshape(nb * tt, d)
    pair = 2 * HEAD_DIM
    d_rec = rg_ref.shape[1] // 2
    parts = []
    for t0 in range(0, tt, HALO):
        rows = hn3[:, t0:t0 + HALO, :].reshape(nb * HALO, d)
        parts.append(jnp.dot(p_ref[...], rows, preferred_element_type=F32).astype(BF16))
    hn_t = jnp.concatenate(parts, axis=0)
    gate = jnp.dot(hn_t, w_ref[:, 3 * D_ATT + d_rec:], preferred_element_type=F32)
    rg_ref[:, d_rec:] = _gelu(gate)
    rg_ref[:, :d_rec] = jnp.dot(hn_t, w_ref[:, 3 * D_ATT:3 * D_ATT + d_rec], preferred_element_type=F32)
    q = jnp.dot(hn, w_ref[:, :D_ATT], preferred_element_type=F32)
    qk_ref[:, :, :D_ATT] = (q * (HEAD_DIM ** -0.5)).astype(BF16).reshape(nb, tt, D_ATT)
    k = jnp.dot(hn, w_ref[:, D_ATT:2 * D_ATT], preferred_element_type=F32)
    qk_ref[:, :, D_ATT:] = k.astype(BF16).reshape(nb, tt, D_ATT)
    v = jnp.dot(hn, w_ref[:, 2 * D_ATT:3 * D_ATT], preferred_element_type=F32).astype(BF16)
    ones = jnp.ones((nb, tt, pair), BF16)
    for p in range(D_ATT // pair):
        vx_ref[:, :, 2 * p * pair:(2 * p + 1) * pair] = v[:, p * pair:(p + 1) * pair].reshape(nb, tt, pair)
        vx_ref[:, :, (2 * p + 1) * pair:(2 * p + 2) * pair] = ones


def _in_proj_first_kernel(x_ref, g_ref, p_ref, w_ref, qk_ref, vx_ref, rg_ref):
    _in_proj_body(_rms(x_ref[...], g_ref[...]).astype(BF16), p_ref, w_ref, qk_ref, vx_ref, rg_ref)


def _in_proj_kernel(h_ref, p_ref, w_ref, qk_ref, vx_ref, rg_ref):
    _in_proj_body(h_ref[...], p_ref, w_ref, qk_ref, vx_ref, rg_ref)


def _in_proj(h, w_all, layer, g_first=None, tt=64):
    bsz, s, d = h.shape
    d_in = w_all.shape[2]
    d_rg = d_in - 3 * D_ATT
    perm = _row_permutation(HALO, bsz)
    bm_tile = pl.BlockSpec((bsz, tt, 2 * D_ATT), lambda i: (0, i, 0))
    h_spec = pl.BlockSpec((bsz, tt, d), lambda i: (0, i, 0))
    rest = [pl.BlockSpec(perm.shape, lambda i: (0, 0)),
            pl.BlockSpec((None, d, d_in), lambda i: (layer, 0, 0))]
    if g_first is None:
        body, in_specs, args = _in_proj_kernel, [h_spec] + rest, (h, perm, w_all)
    else:
        body = _in_proj_first_kernel
        in_specs = [h_spec, pl.BlockSpec((1, d), lambda i: (0, 0))] + rest
        args = (h, g_first.reshape(1, d), perm, w_all)
    return pl.pallas_call(
        body,
        grid=(s // tt,),
        in_specs=in_specs,
        out_specs=[bm_tile, bm_tile, pl.BlockSpec((tt * bsz, d_rg), lambda i: (i, 0))],
        out_shape=[
            jax.ShapeDtypeStruct((bsz, s, 2 * D_ATT), BF16),
            jax.ShapeDtypeStruct((bsz, s, 2 * D_ATT), BF16),
            jax.ShapeDtypeStruct((s * bsz, d_rg), F32),
        ],
        compiler_params=_params(("parallel",)),
        name="in_proj",
    )(*args)


def _bias_rows(rel_bias):
    h = rel_bias.shape[0]
    z = jnp.zeros((h, 2 * WIN_H - 2, 33), rel_bias.dtype)
    return jnp.concatenate(
        [rel_bias[:, :-1, WIN_W - 1:], z, rel_bias[:, 1:, :], z, rel_bias[:, :-1, :WIN_W - 1]], axis=-1)


def _bias_kernel(rows_ref, o_ref):
    n_dr = o_ref.shape[0]
    shape = (GRID_W, 2 * GRID_W)
    c = lax.broadcasted_iota(jnp.int32, shape, 0)
    kc = lax.broadcasted_iota(jnp.int32, shape, 1) & (GRID_W - 1)
    col_start = jnp.clip(c - WIN_W // 2, 0, GRID_W - WIN_W)
    valid = (kc >= col_start) & (kc < col_start + WIN_W)
    for h in range(N_HEADS):
        for dr in range(n_dr):
            row = jnp.broadcast_to(rows_ref[h, dr:dr + 1, :], shape)
            toeplitz = pltpu.roll(row, 0, 1, stride=1, stride_axis=0)
            o_ref[dr, h * GRID_W:(h + 1) * GRID_W, :] = jnp.where(valid, toeplitz, MASKED)


def _bias_pieces(rel_bias):
    rows = _bias_rows(rel_bias.astype(F32))
    n_dr = rows.shape[1]
    return pl.pallas_call(
        _bias_kernel,
        out_shape=jax.ShapeDtypeStruct((n_dr, N_HEADS * GRID_W, 2 * GRID_W), F32),
        name="bias_pieces",
    )(rows)


def _attn_kernel(q_ref, k_ref, v_ref, bias_ref, g_ref, o_ref, *, rows, rows_per_step, rows_per_iter):
    i = pl.program_id(1)
    band = WIN_H * GRID_W
    pair = 2 * HEAD_DIM
    group = 4 * HEAD_DIM
    n_groups = D_ATT // group
    lane = lax.broadcasted_iota(jnp.int32, (GRID_W, group), 1)
    head_lanes = [(lane >= h * HEAD_DIM) & (lane < (h + 1) * HEAD_DIM) for h in range(4)]
    low_half = lax.broadcasted_iota(jnp.int32, (GRID_W, pair), 1) < HEAD_DIM
    g = g_ref[...]

    def qk_stage(r, q0):
        row_start = jnp.clip(r - WIN_H // 2, 0, rows - WIN_H)
        off = r - row_start
        k0 = pl.multiple_of(row_start * GRID_W, GRID_W)
        scores = []
        for gi in range(n_groups):
            lanes = pl.ds(gi * group, group)
            q4 = q_ref[0, pl.ds(q0, GRID_W), lanes]
            qm = jnp.concatenate([jnp.where(m, q4, jnp.zeros_like(q4)) for m in head_lanes], axis=0)
            k4 = k_ref[0, pl.ds(k0, band), lanes]
            sc = lax.dot_general(qm, k4, (((1,), (1,)), ((), ())), preferred_element_type=F32)
            bias = jnp.concatenate(
                [bias_ref[(WIN_H - 1) - off + 2 * ii, pl.ds(gi * group, group), :]
                 for ii in range(WIN_H // 2)], axis=1)
            scores.append(sc + bias)
        return k0, scores

    def softmax_stage(scores):
        return [jnp.exp(sc - jnp.max(sc, axis=-1, keepdims=True)).astype(BF16) for sc in scores]

    def pv_stage(k0, q0, probs):
        outs = []
        for gi in range(n_groups):
            for pp in range(2):
                e2 = probs[gi][pp * pair:(pp + 1) * pair, :]
                vx = v_ref[0, pl.ds(k0, band), pl.ds((2 * gi + pp) * 2 * pair, 2 * pair)]
                pv = jnp.dot(e2, vx, preferred_element_type=F32)
                o = pv[:, :pair] / pv[:, pair:]
                outs.append(jnp.where(low_half, o[:GRID_W], o[GRID_W:]))
        att = jnp.concatenate(outs, axis=-1)
        o_ref[0, pl.ds(q0, GRID_W), :] = _rms(att, g).astype(o_ref.dtype)

    def iter_body(j, carry):
        base = j * rows_per_iter
        q0s = [pl.multiple_of((base + t) * GRID_W, GRID_W) for t in range(rows_per_iter)]
        k0s, scs, probs = {}, {}, {}
        for t in range(rows_per_iter + 2):
            if t < rows_per_iter:
                k0s[t], scs[t] = qk_stage(i * rows_per_step + base + t, q0s[t])
            if 1 <= t <= rows_per_iter:
                probs[t - 1] = softmax_stage(scs.pop(t - 1))
            if t >= 2:
                pv_stage(k0s[t - 2], q0s[t - 2], probs.pop(t - 2))
        return carry

    lax.fori_loop(0, rows_per_step // rows_per_iter, iter_body, 0)


def _attention(qk, vx, bias, g_att, rows_per_step=16, rows_per_iter=8):
    bsz, s, _ = qk.shape
    rows = s // GRID_W
    tq = rows_per_step * GRID_W
    kern = functools.partial(_attn_kernel, rows=rows, rows_per_step=rows_per_step,
                             rows_per_iter=rows_per_iter)
    return pl.pallas_call(
        kern,
        grid=(bsz, rows // rows_per_step),
        in_specs=[
            pl.BlockSpec((1, tq, D_ATT), lambda b, i: (b, i, 0)),
            pl.BlockSpec((1, s, D_ATT), lambda b, i: (b, 0, 1)),
            pl.BlockSpec((1, s, 2 * D_ATT), lambda b, i: (b, 0, 0)),
            pl.BlockSpec(bias.shape, lambda b, i: (0, 0, 0)),
            pl.BlockSpec((1, D_ATT), lambda b, i: (0, 0)),
        ],
        out_specs=pl.BlockSpec((1, tq, D_ATT), lambda b, i: (b, i, 0)),
        out_shape=jax.ShapeDtypeStruct((bsz, s, D_ATT), BF16),
        compiler_params=_params(("parallel", "arbitrary")),
        name="nbr_attention",
    )(qk, qk, vx, bias, g_att.reshape(1, D_ATT))


def _lru_au(conv_rows, a_ref, u_ref, wg_ref, bg_ref, lam_ref, *, bsz, tc, sub_t):
    c = a_ref.shape[1]
    z = -lam_ref[...]
    neg_c_softplus = -LRU_C * (jnp.maximum(z, 0.0) + jnp.log1p(jnp.exp(-jnp.abs(z))))
    sub = sub_t * bsz

    def body(sb, carry):
        r0 = pl.multiple_of(sb * sub, sub)
        xc = conv_rows(r0, sub)
        gates = jnp.dot(xc.astype(BF16), wg_ref[...], preferred_element_type=F32) + bg_ref[...]
        r_gate = _sigmoid(gates[:, :c])
        i_gate = _sigmoid(gates[:, c:])
        log_a = r_gate * neg_c_softplus
        a = jnp.exp(log_a)
        a_ref[pl.ds(r0, sub), :] = a
        one_minus_a2 = -jnp.tanh(log_a) * (a * a + 1.0)
        root = jnp.where(one_minus_a2 > 0.0, one_minus_a2 * lax.rsqrt(one_minus_a2), 0.0)
        u_ref[pl.ds(r0, sub), :] = root * (i_gate * xc)
        return carry

    lax.fori_loop(0, tc // sub_t, body, 0, unroll=2)


def _fill_halo(xe_ref, xp_ref, xc_ref, xn_ref, has_prev, has_next, *, bsz, tc, pad):
    left, right = pad
    hb = xp_ref.shape[0]
    prev = xp_ref[pl.ds(hb - left * bsz, left * bsz), :]
    xe_ref[pl.ds(0, left * bsz), :] = jnp.where(has_prev, prev, 0.0)
    xe_ref[pl.ds(left * bsz, tc * bsz), :] = xc_ref[...]
    nxt = xn_ref[pl.ds(0, right * bsz), :]
    xe_ref[pl.ds((left + tc) * bsz, right * bsz), :] = jnp.where(has_next, nxt, 0.0)


def _lru_fwd_kernel(xp_ref, xc_ref, xn_ref, cw_ref, cb_ref, wg_ref, bg_ref, lam_ref,
                    hf_ref, conv_ref, xe_ref, a_ref, u_ref, h_ref, *, bsz, tc, sub_t, pad, unroll):
    i = pl.program_id(0)
    nc = pl.num_programs(0)

    @pl.when(i == 0)
    def _():
        h_ref[...] = jnp.zeros_like(h_ref)

    _fill_halo(xe_ref, xp_ref, xc_ref, xn_ref, i > 0, i < nc - 1, bsz=bsz, tc=tc, pad=pad)

    def conv_rows(r0, n):
        xc = cb_ref[...] + cw_ref[0:1, :] * xe_ref[pl.ds(r0, n), :]
        for j in range(1, cw_ref.shape[0]):
            xc = xc + cw_ref[j:j + 1, :] * xe_ref[pl.ds(r0 + j * bsz, n), :]
        conv_ref[pl.ds(r0, n), :] = xc
        return xc

    _lru_au(conv_rows, a_ref, u_ref, wg_ref, bg_ref, lam_ref, bsz=bsz, tc=tc, sub_t=sub_t)

    def step(t, h):
        rows = pl.ds(pl.multiple_of(t * bsz, bsz), bsz)
        h = a_ref[rows, :] * h + u_ref[rows, :]
        hf_ref[rows, :] = h
        return h

    h_ref[...] = lax.fori_loop(0, tc, step, h_ref[...], unroll=unroll)


def _lru_bwd_kernel(conv_ref, gate_ref, hf_ref, wg_ref, bg_ref, lam_ref, grec_ref, perm_ref,
                    o_ref, a_ref, u_ref, hb_ref, h_ref, *, bsz, tc, sub_t, unroll):
    i = pl.program_id(0)

    @pl.when(i == 0)
    def _():
        h_ref[...] = jnp.zeros_like(h_ref)

    _lru_au(lambda r0, n: conv_ref[pl.ds(r0, n), :], a_ref, u_ref, wg_ref, bg_ref, lam_ref,
            bsz=bsz, tc=tc, sub_t=sub_t)

    def step(k, h):
        t = tc - 1 - k
        rows = pl.ds(pl.multiple_of(t * bsz, bsz), bsz)
        h = a_ref[rows, :] * h + u_ref[rows, :]
        hb_ref[rows, :] = h
        return h

    h_ref[...] = lax.fori_loop(0, tc, step, h_ref[...], unroll=unroll)

    sub = sub_t * bsz

    def finish(sb, carry):
        rows = pl.ds(pl.multiple_of(sb * sub, sub), sub)
        rec = (hf_ref[rows, :] + hb_ref[rows, :]) * gate_ref[rows, :]
        rec_n = _rms(rec, grec_ref[...]).astype(o_ref.dtype)
        rec_b = jnp.dot(perm_ref[...], rec_n, preferred_element_type=F32).astype(o_ref.dtype)
        t0 = pl.multiple_of(sb * sub_t, sub_t)
        for b in range(bsz):
            o_ref[b, pl.ds(t0, sub_t), :] = rec_b[b * sub_t:(b + 1) * sub_t]
        return carry

    lax.fori_loop(0, tc // sub_t, finish, 0, unroll=4)


def _lru(rg_t, cw, cb, wg, bg, lam, g_rec, *, bsz, tc=256, sub_t=32, unroll=8):
    rows, c2 = rg_t.shape
    c = c2 // 2
    s = rows // bsz
    nc = s // tc
    k = cw.shape[0]
    pad = (k // 2, k - 1 - k // 2)
    halo_t = 8
    hb = halo_t * bsz
    n_halo = s // halo_t
    cpb = tc // halo_t
    common = dict(bsz=bsz, tc=tc, sub_t=sub_t, unroll=unroll)
    chunk_f32 = pltpu.VMEM((tc * bsz, c), F32)
    carry = pltpu.VMEM((bsz, c), F32)
    row = lambda width: pl.BlockSpec((1, width), lambda i: (0, 0))
    gate_specs = [pl.BlockSpec((c, 2 * c), lambda i: (0, 0)), row(2 * c), row(c)]

    fwd_chunk = pl.BlockSpec((tc * bsz, c), lambda i: (i, 0))
    hf, conv = pl.pallas_call(
        functools.partial(_lru_fwd_kernel, pad=pad, **common),
        grid=(nc,),
        in_specs=[
            pl.BlockSpec((hb, c), lambda i: (jnp.maximum(i * cpb - 1, 0), 0)),
            fwd_chunk,
            pl.BlockSpec((hb, c), lambda i: (jnp.minimum((i + 1) * cpb, n_halo - 1), 0)),
            pl.BlockSpec((k, c), lambda i: (0, 0)),
            row(c),
        ] + gate_specs,
        out_specs=[fwd_chunk, fwd_chunk],
        out_shape=[jax.ShapeDtypeStruct((rows, c), F32), jax.ShapeDtypeStruct((rows, c), F32)],
        scratch_shapes=[pltpu.VMEM(((tc + k - 1) * bsz, c), F32), chunk_f32, chunk_f32, carry],
        compiler_params=_params(("arbitrary",)),
        name="lru_fwd",
    )(rg_t, rg_t, rg_t, cw, cb.reshape(1, c), wg[0], bg[0].reshape(1, 2 * c), lam[0].reshape(1, c))

    perm = _row_permutation(bsz, sub_t)
    bwd_chunk = pl.BlockSpec((tc * bsz, c), lambda i: (nc - 1 - i, 0))
    return pl.pallas_call(
        functools.partial(_lru_bwd_kernel, **common),
        grid=(nc,),
        in_specs=[
            bwd_chunk,
            pl.BlockSpec((tc * bsz, c), lambda i: (nc - 1 - i, 1)),
            bwd_chunk,
        ] + gate_specs + [row(c), pl.BlockSpec(perm.shape, lambda i: (0, 0))],
        out_specs=pl.BlockSpec((bsz, tc, c), lambda i: (0, nc - 1 - i, 0)),
        out_shape=jax.ShapeDtypeStruct((bsz, s, c), BF16),
        scratch_shapes=[chunk_f32, chunk_f32, chunk_f32, carry],
        compiler_params=_params(("arbitrary",)),
        name="lru_bwd",
    )(conv, rg_t, hf, wg[1], bg[1].reshape(1, 2 * c), lam[1].reshape(1, c), g_rec.reshape(1, c), perm)


def _gate_weights(lru_w):
    nd, ng, nb, bs, _ = lru_w.shape
    eye = jnp.eye(nb, dtype=lru_w.dtype)
    dense = jnp.einsum("xgncd,nm->xgncmd", lru_w, eye).reshape(nd, ng, nb * bs, nb * bs)
    return jnp.concatenate([dense[:, 0], dense[:, 1]], axis=-1)


def _out_proj_kernel(x_ref, att_ref, rec_ref, wo_ref, g_ref, x1_ref, hn_ref):
    mixed = jnp.concatenate([att_ref[...], rec_ref[...]], axis=-1)
    x1 = x_ref[...] + jnp.dot(mixed, wo_ref[...], preferred_element_type=F32)
    x1_ref[...] = x1
    hn_ref[...] = _rms(x1, g_ref[...]).astype(hn_ref.dtype)


def _out_proj(x, att, rec, wo_all, layer, g, tm=1024):
    n, d = x.shape
    c = att.shape[1]
    return pl.pallas_call(
        _out_proj_kernel,
        grid=(n // tm,),
        in_specs=[
            pl.BlockSpec((tm, d), lambda i: (i, 0)),
            pl.BlockSpec((tm, c), lambda i: (i, 0)),
            pl.BlockSpec((tm, c), lambda i: (i, 0)),
            pl.BlockSpec((None,) + wo_all.shape[1:], lambda i: (layer, 0, 0)),
            pl.BlockSpec((1, d), lambda i: (0, 0)),
        ],
        out_specs=[
            pl.BlockSpec((tm, d), lambda i: (i, 0)),
            pl.BlockSpec((tm, d), lambda i: (i, 0)),
        ],
        out_shape=[
            jax.ShapeDtypeStruct((n, d), F32),
            jax.ShapeDtypeStruct((n, d), BF16),
        ],
        compiler_params=_params(("parallel",)),
        name="out_proj",
    )(x, att, rec, wo_all, g.reshape(1, d))


def _ffn_kernel(hc_ref, hp_ref, hx_ref, x1_ref, wu_ref, cw_ref, cb_ref, wd_ref, g_ref, *out_refs,
                n_chunks, sub_rows, last):
    j = pl.program_id(1)
    nj = pl.num_programs(1)
    tm = hc_ref.shape[1]
    f = wd_ref.shape[0]
    wc = f // n_chunks
    n_sub = tm // sub_rows
    prev = jnp.where(j > 0, hp_ref[0], jnp.zeros_like(hp_ref[0]))
    nxt = jnp.where(j < nj - 1, hx_ref[0], jnp.zeros_like(hx_ref[0]))
    hn_ext = jnp.concatenate([prev, hc_ref[0], nxt], axis=0)
    ext = sub_rows + 2 * HALO
    gelu_c = math.sqrt(2.0 / math.pi)

    def up_proj(stage):
        k, ch = divmod(stage, n_chunks)
        rows_ext = hn_ext[k * sub_rows:k * sub_rows + ext]
        rows = hn_ext[k * sub_rows + HALO:(k + 1) * sub_rows + HALO]
        return (jnp.dot(rows_ext, wu_ref[:, pl.ds(ch * wc, wc)], preferred_element_type=F32),
                jnp.dot(rows, wu_ref[:, pl.ds(f + ch * wc, wc)], preferred_element_type=F32))

    n_stages = n_sub * n_chunks
    pending = up_proj(0)
    acc = None
    for stage in range(n_stages):
        k, ch = divmod(stage, n_chunks)
        rows = pl.ds(k * sub_rows, sub_rows)
        cols = pl.ds(ch * wc, wc)
        ua_ext, ul = pending
        if stage + 1 < n_stages:
            pending = up_proj(stage + 1)
        if ch == 0:
            acc = x1_ref[0, rows, :]
        before = pltpu.roll(ua_ext, 1, axis=0)[HALO:HALO + sub_rows]
        after = pltpu.roll(ua_ext, ext - 1, axis=0)[HALO:HALO + sub_rows]
        conv = (cb_ref[:, cols] + cw_ref[0:1, cols] * before
                + cw_ref[1:2, cols] * ua_ext[HALO:HALO + sub_rows] + cw_ref[2:3, cols] * after)
        inner = conv * (conv * conv * (gelu_c * 0.044715) + gelu_c)
        z = (0.5 * conv) * ul
        act = (z + z * jnp.tanh(inner)).astype(BF16)
        acc = acc + jnp.dot(act, wd_ref[cols, :], preferred_element_type=F32)
        if ch == n_chunks - 1:
            if last:
                out_refs[0][0, rows, :] = _rms(acc, g_ref[...])
            else:
                out_refs[0][0, rows, :] = acc
                out_refs[1][0, rows, :] = _rms(acc, g_ref[...]).astype(out_refs[1].dtype)


def _ffn(hn, x1, wu_all, cw, cb, wd_all, layer, g_next, *, last, tm=512, sub_rows=256):
    bsz, s, d = x1.shape
    f = wd_all.shape[1]
    tpb = tm // HALO
    n_hb = s // HALO
    tile = pl.BlockSpec((1, tm, d), lambda b, j: (b, j, 0))
    if last:
        out_specs = [tile]
        out_shape = [jax.ShapeDtypeStruct((bsz, s, d), F32)]
    else:
        out_specs = [tile, tile]
        out_shape = [jax.ShapeDtypeStruct((bsz, s, d), F32), jax.ShapeDtypeStruct((bsz, s, d), BF16)]
    return pl.pallas_call(
        functools.partial(_ffn_kernel, n_chunks=6, sub_rows=sub_rows, last=last),
        grid=(bsz, s // tm),
        in_specs=[
            tile,
            pl.BlockSpec((1, HALO, d), lambda b, j: (b, jnp.maximum(j * tpb - 1, 0), 0)),
            pl.BlockSpec((1, HALO, d), lambda b, j: (b, jnp.minimum((j + 1) * tpb, n_hb - 1), 0)),
            tile,
            pl.BlockSpec((None,) + wu_all.shape[1:], lambda b, j: (layer, 0, 0)),
            pl.BlockSpec(cw.shape, lambda b, j: (0, 0)),
            pl.BlockSpec((1, f), lambda b, j: (0, 0)),
            pl.BlockSpec((None,) + wd_all.shape[1:], lambda b, j: (layer, 0, 0)),
            pl.BlockSpec((1, d), lambda b, j: (0, 0)),
        ],
        out_specs=out_specs,
        out_shape=out_shape,
        compiler_params=_params(("parallel", "arbitrary")),
        name="ffn",
    )(hn, hn, hn, x1, wu_all, cw, cb.reshape(1, f), wd_all, g_next.reshape(1, d))


def kernel(x, g_mix, w_in, rel_bias, conv_lru_w, conv_lru_b, lru_w, lru_b, lru_lam, g_att, g_rec,
           w_out, g_ffn, w_up, conv_ffn_w, conv_ffn_b, w_down, g_final):
    bsz, s, d = x.shape
    depth = w_in.shape[0]
    n = bsz * s
    c_lru = conv_lru_w.shape[-1]
    w_in_b = w_in.astype(BF16)
    w_out_b = w_out.astype(BF16)
    w_up_b = w_up.astype(BF16)
    w_down_b = w_down.astype(BF16)
    hn = x
    for l in range(depth):
        last = l == depth - 1
        qk, vx, rg_t = _in_proj(hn, w_in_b, l, g_first=g_mix[0] if l == 0 else None)
        att = _attention(qk, vx, _bias_pieces(rel_bias[l]), g_att[l])
        wg = _gate_weights(lru_w[l]).astype(BF16)
        bg = jnp.concatenate([lru_b[l, :, 0], lru_b[l, :, 1]], axis=-1)
        rec = _lru(rg_t, conv_lru_w[l], conv_lru_b[l], wg, bg, lru_lam[l], g_rec[l], bsz=bsz)
        x1, hf = _out_proj(x.reshape(n, d), att.reshape(n, D_ATT), rec.reshape(n, c_lru), w_out_b, l,
                           g_ffn[l])
        outs = _ffn(hf.reshape(bsz, s, d), x1.reshape(bsz, s, d), w_up_b, conv_ffn_w[l],
                    conv_ffn_b[l], w_down_b, l, g_final if last else g_mix[l + 1], last=last)
        if last:
            return outs[0]
        x, hn = outs
```
